```python
import math
import numpy as np
import jax
import jax.numpy as jnp
from jax import lax

D_MODEL = 2048
BATCH = 8
SEQ = 2048
DEPTH = 1
DEC_BATCH = 1
DEC_SEQ = 8192
PAST_LEN = 128

DA_HEADS = 8
DA_HEAD_DIM = 64
DA_V_DIM = 2 * DA_HEAD_DIM
DA_QK_W = DA_HEADS * 2 * DA_HEAD_DIM
DA_V_W = DA_HEADS * DA_V_DIM
Q_BLOCK = 128
GDN_HEADS = 8
GDN_K_DIM = 128
GDN_V_DIM = 128
GDN_K_W = GDN_HEADS * GDN_K_DIM
GDN_V_W = GDN_HEADS * GDN_V_DIM
GDN_CHUNK = 64
CONV_K = 5
FFN_HIDDEN = ((8 * D_MODEL // 3 + 255) // 256) * 256
PLE_DIM = 256
N_BRANCH = 2
PROJ_SPLIT = (DA_QK_W, DA_QK_W, DA_V_W, GDN_K_W, GDN_K_W, GDN_V_W, GDN_V_W,
              2 * GDN_HEADS, 2 * GDN_HEADS, N_BRANCH * D_MODEL)
PROJ_W = sum(PROJ_SPLIT)
NORM_EPS = 1e-6

kernel_name = 'hybrid_diffattn_gdn_encoder'


def rms_norm(x, gain):
    xf = x.astype(jnp.float32)
    y = xf * lax.rsqrt(jnp.mean(xf * xf, axis=-1, keepdims=True) + NORM_EPS)
    return (y * gain.astype(jnp.float32)).astype(x.dtype)


def l2_norm(x):
    return x * lax.rsqrt(jnp.sum(x * x, axis=-1, keepdims=True) + NORM_EPS)


def dwconv_centred(x, w):
    k, c = w.shape
    return lax.conv_general_dilated(
        x, w[:, None, :], window_strides=(1,), padding=[((k - 1) // 2, k // 2)],
        dimension_numbers=('NWC', 'WIO', 'NWC'), feature_group_count=c)


def diff_softmax_attention(q, k, v, lam):
    _, b, h, s, d = q.shape
    nb = s // Q_BLOCK
    scale = d ** -0.5
    slopes = 2.0 ** (-8.0 * jnp.arange(1, h + 1, dtype=jnp.float32) / h)
    kpos = jnp.arange(s, dtype=jnp.float32)
    q_blocks = q.reshape(2, b, h, nb, Q_BLOCK, d).transpose(3, 0, 1, 2, 4, 5)
    starts = jnp.arange(nb, dtype=jnp.float32) * Q_BLOCK

    def block(args):
        qb, t0 = args
        qpos = t0 + jnp.arange(Q_BLOCK, dtype=jnp.float32)
        bias = -slopes[:, None, None] * jnp.abs(qpos[:, None] - kpos[None, :])
        sc = jnp.einsum('mbhqd,mbhkd->mbhqk', qb, k) * scale + bias
        pr = jax.nn.softmax(sc, axis=-1)
        wts = pr[0] - lam * pr[1]
        return jnp.einsum('bhqk,bhkd->bhqd', wts, v)

    out = lax.map(block, (q_blocks, starts))
    return out.transpose(1, 2, 0, 3, 4).reshape(b, h, s, v.shape[-1])


def diff_attn_branch(q, k, v, q_gain, k_gain, lq1, lk1, lq2, lk2, subln, lam_init):
    b, s, _ = q.shape
    out_dtype = q.dtype
    q = rms_norm(q.reshape(b, s, DA_HEADS, 2, DA_HEAD_DIM), q_gain).astype(jnp.float32).transpose(3, 0, 2, 1, 4)
    k = rms_norm(k.reshape(b, s, DA_HEADS, 2, DA_HEAD_DIM), k_gain).astype(jnp.float32).transpose(3, 0, 2, 1, 4)
    v = v.reshape(b, s, DA_HEADS, DA_V_DIM).transpose(0, 2, 1, 3).astype(jnp.float32)
    lam = (jnp.exp(jnp.sum(lq1.astype(jnp.float32) * lk1.astype(jnp.float32)))
           - jnp.exp(jnp.sum(lq2.astype(jnp.float32) * lk2.astype(jnp.float32))) + lam_init)
    o = diff_softmax_attention(q, k, v, lam)
    o = rms_norm(o, subln) * (1.0 - lam_init)
    return o.transpose(0, 2, 1, 3).reshape(b, s, DA_V_W).astype(out_dtype)


def gdn_chunked(q, k, v, g, beta):
    b, h, s, dk = q.shape
    dv = v.shape[-1]
    n, c = s // GDN_CHUNK, GDN_CHUNK
    q = q.reshape(b, h, n, c, dk)
    k = k.reshape(b, h, n, c, dk)
    v = v.reshape(b, h, n, c, dv)
    gc = jnp.cumsum(g.reshape(b, h, n, c), axis=-1)
    beta = beta.reshape(b, h, n, c)
    tri = jnp.tril(jnp.ones((c, c), dtype=bool))
    strict = jnp.tril(jnp.ones((c, c), dtype=bool), -1)
    decay = jnp.exp(jnp.where(tri, gc[..., :, None] - gc[..., None, :], -jnp.inf))
    kb = k * beta[..., None]
    vb = v * beta[..., None]
    a_mat = jnp.where(strict, jnp.einsum('bhnik,bhnjk->bhnij', kb, k) * decay, 0.0)
    lower = a_mat + jnp.eye(c, dtype=a_mat.dtype)
    rhs = jnp.concatenate([vb, kb * jnp.exp(gc)[..., None]], axis=-1)
    sol = lax.linalg.triangular_solve(lower, rhs, left_side=True, lower=True, unit_diagonal=True)
    u, w = sol[..., :dv], sol[..., dv:]
    qk = jnp.where(tri, jnp.einsum('bhnik,bhnjk->bhnij', q, k) * decay, 0.0)
    g_last = gc[..., -1]
    qg = q * jnp.exp(gc)[..., None]
    kdec = k * jnp.exp(g_last[..., None] - gc)[..., None]

    def step(state, xs):
        qg_i, qk_i, u_i, w_i, kdec_i, gl_i = xs
        v_new = u_i - jnp.einsum('bhck,bhkv->bhcv', w_i, state)
        o = jnp.einsum('bhck,bhkv->bhcv', qg_i, state) + jnp.einsum('bhcj,bhjv->bhcv', qk_i, v_new)
        state = state * jnp.exp(gl_i)[..., None, None] + jnp.einsum('bhck,bhcv->bhkv', kdec_i, v_new)
        return state, o

    xs = tuple(jnp.moveaxis(t, 2, 0) for t in (qg, qk, u, w, kdec, g_last))
    state0 = jnp.zeros((b, h, dk, dv), jnp.float32)
    _, o = lax.scan(step, state0, xs)
    return jnp.moveaxis(o, 0, 2).reshape(b, h, s, dv)


def gdn_branch(q, k, v, z, a, bgate, conv_w, a_log, dt_bias, out_norm):
    b, s, _ = q.shape
    out_dtype = q.dtype
    qkv = jax.nn.silu(dwconv_centred(jnp.concatenate([q, k, v], axis=-1), conv_w))
    q, k, v = jnp.split(qkv, [GDN_K_W, 2 * GDN_K_W], axis=-1)

    def heads(t, d):
        return t.reshape(b, s, GDN_HEADS, d).transpose(0, 2, 1, 3).astype(jnp.float32)

    q = l2_norm(heads(q, GDN_K_DIM)) * (GDN_K_DIM ** -0.5)
    k = l2_norm(heads(k, GDN_K_DIM))
    v = heads(v, GDN_V_DIM)
    a = a.astype(jnp.float32).reshape(b, s, 2, GDN_HEADS).transpose(2, 0, 3, 1)
    bgate = bgate.astype(jnp.float32).reshape(b, s, 2, GDN_HEADS).transpose(2, 0, 3, 1)
    g = -jnp.exp(a_log.astype(jnp.float32))[:, None, :, None] * jax.nn.softplus(
        a + dt_bias.astype(jnp.float32)[:, None, :, None])
    beta = jax.nn.sigmoid(bgate)
    o_fwd = gdn_chunked(q, k, v, g[0], beta[0])
    flip = lambda t: jnp.flip(t, axis=2)
    o_bwd = flip(gdn_chunked(flip(q), flip(k), flip(v), flip(g[1]), flip(beta[1])))
    o = (o_fwd + o_bwd).transpose(0, 2, 1, 3)
    o = rms_norm(o, out_norm) * jax.nn.silu(z.reshape(b, s, GDN_HEADS, GDN_V_DIM).astype(jnp.float32))
    return o.reshape(b, s, GDN_V_W).astype(out_dtype)


def encoder_layer(x, p, lam_init, prm):
    h = rms_norm(x, prm['norm_mix'])
    proj = h @ prm['w_in']
    da_q, da_k, da_v, g_q, g_k, g_v, g_z, g_a, g_b, gates = jnp.split(
        proj, np.cumsum(PROJ_SPLIT)[:-1].tolist(), axis=-1)
    ya = diff_attn_branch(da_q, da_k, da_v, prm['da_q_norm'], prm['da_k_norm'],
                          prm['lambda_q1'], prm['lambda_k1'], prm['lambda_q2'], prm['lambda_k2'],
                          prm['da_subln'], lam_init)
    yb = gdn_branch(g_q, g_k, g_v, g_z, g_a, g_b, prm['gdn_conv'], prm['gdn_a_log'],
                    prm['gdn_dt_bias'], prm['gdn_out_norm'])
    gate_a, gate_b = jnp.split(jax.nn.sigmoid(gates), 2, axis=-1)
    mixed = gate_a * (ya @ prm['w_branch_a']) + gate_b * (yb @ prm['w_branch_b'])
    x = x + mixed @ prm['w_out']
    h = rms_norm(x, prm['norm_ffn'])
    up, gt = jnp.split(h @ prm['w_ffn_in'], 2, axis=-1)
    x = x + (jax.nn.silu(gt) * up) @ prm['w_ffn_out']
    h = rms_norm(x, prm['norm_ple'])
    x = x + jax.nn.sigmoid(h @ prm['w_ple_gate']) * (p @ prm['w_ple_proj'])
    return x


def setup_inputs(seed: int = 0) -> dict:
    key = jax.random.key(seed)
    kit = iter(jax.random.split(key, 40))
    f32 = jnp.float32

    def nrm(shape, scale):
        return jax.random.normal(next(kit), shape, f32) * scale

    def gain(shape):
        return 1.0 + 0.05 * jax.random.normal(next(kit), shape, f32)

    dt = jnp.exp(jax.random.uniform(next(kit), (DEPTH, 2, GDN_HEADS), f32,
                                    math.log(1e-3), math.log(1e-1)))
    return {
        'x_prompt': nrm((BATCH, SEQ, D_MODEL), 1.0),
        'x_sample': nrm((DEC_BATCH, DEC_SEQ, D_MODEL), 1.0),
        'p_prompt': nrm((DEPTH, BATCH, SEQ, PLE_DIM), 1.0),
        'p_sample': nrm((DEPTH, DEC_BATCH, DEC_SEQ, PLE_DIM), 1.0),
        'norm_mix': gain((DEPTH, D_MODEL)),
        'w_in': nrm((DEPTH, D_MODEL, PROJ_W), D_MODEL ** -0.5),
        'da_q_norm': gain((DEPTH, DA_HEAD_DIM)),
        'da_k_norm': gain((DEPTH, DA_HEAD_DIM)),
        'lambda_q1': nrm((DEPTH, DA_HEAD_DIM), 0.1),
        'lambda_k1': nrm((DEPTH, DA_HEAD_DIM), 0.1),
        'lambda_q2': nrm((DEPTH, DA_HEAD_DIM), 0.1),
        'lambda_k2': nrm((DEPTH, DA_HEAD_DIM), 0.1),
        'da_subln': gain((DEPTH, DA_V_DIM)),
        'gdn_conv': nrm((DEPTH, CONV_K, 2 * GDN_K_W + GDN_V_W), CONV_K ** -0.5),
        'gdn_a_log': jnp.log(jax.random.uniform(next(kit), (DEPTH, 2, GDN_HEADS), f32, 1.0, 16.0)),
        'gdn_dt_bias': dt + jnp.log(-jnp.expm1(-dt)),
        'gdn_out_norm': gain((DEPTH, GDN_V_DIM)),
        'w_branch_a': nrm((DEPTH, DA_V_W, D_MODEL), DA_V_W ** -0.5),
        'w_branch_b': nrm((DEPTH, GDN_V_W, D_MODEL), GDN_V_W ** -0.5),
        'w_out': nrm((DEPTH, D_MODEL, D_MODEL), D_MODEL ** -0.5),
        'norm_ffn': gain((DEPTH, D_MODEL)),
        'w_ffn_in': nrm((DEPTH, D_MODEL, 2 * FFN_HIDDEN), D_MODEL ** -0.5),
        'w_ffn_out': nrm((DEPTH, FFN_HIDDEN, D_MODEL), FFN_HIDDEN ** -0.5),
        'norm_ple': gain((DEPTH, D_MODEL)),
        'w_ple_gate': nrm((DEPTH, D_MODEL, D_MODEL), D_MODEL ** -0.5),
        'w_ple_proj': nrm((DEPTH, PLE_DIM, D_MODEL), PLE_DIM ** -0.5),
    }


def reference(x_prompt, x_sample, p_prompt, p_sample, norm_mix, w_in, da_q_norm, da_k_norm,
              lambda_q1, lambda_k1, lambda_q2, lambda_k2, da_subln, gdn_conv, gdn_a_log,
              gdn_dt_bias, gdn_out_norm, w_branch_a, w_branch_b, w_out, norm_ffn, w_ffn_in,
              w_ffn_out, norm_ple, w_ple_gate, w_ple_proj):
    y_prompt = x_prompt
    y_sample = x_sample
    for l in range(DEPTH):
        lam_init = 0.8 - 0.6 * math.exp(-0.3 * l)
        prm = {
            'norm_mix': norm_mix[l], 'w_in': w_in[l],
            'da_q_norm': da_q_norm[l], 'da_k_norm': da_k_norm[l],
            'lambda_q1': lambda_q1[l], 'lambda_k1': lambda_k1[l],
            'lambda_q2': lambda_q2[l], 'lambda_k2': lambda_k2[l],
            'da_subln': da_subln[l], 'gdn_conv': gdn_conv[l],
            'gdn_a_log': gdn_a_log[l], 'gdn_dt_bias': gdn_dt_bias[l],
            'gdn_out_norm': gdn_out_norm[l], 'w_branch_a': w_branch_a[l],
            'w_branch_b': w_branch_b[l], 'w_out': w_out[l], 'norm_ffn': norm_ffn[l],
            'w_ffn_in': w_ffn_in[l], 'w_ffn_out': w_ffn_out[l], 'norm_ple': norm_ple[l],
            'w_ple_gate': w_ple_gate[l], 'w_ple_proj': w_ple_proj[l],
        }
        y_prompt = encoder_layer(y_prompt, p_prompt[l], lam_init, prm)
        y_sample = encoder_layer(y_sample, p_sample[l], lam_init, prm)
    return (y_prompt, y_sample)
```

```python
import functools
import math

import jax
import jax.numpy as jnp
from jax import lax
from jax.experimental import pallas as pl
from jax.experimental.pallas import tpu as pltpu

F32 = jnp.float32
BF16 = jnp.bfloat16

D_MODEL = 2048
DA_HEADS = 8
DA_HEAD_DIM = 64
GDN_HEADS = 8
GDN_DIM = 128
GDN_CHUNK = 64
CONV_K = 5
FFN_HIDDEN = 5632
PLE_DIM = 256
NORM_EPS = 1e-6
LAM_INIT = 0.8 - 0.6 * math.exp(-0.3 * 0)
LOG2E = math.log2(math.e)

LANES = 128
MXU_DIM = 256
GDN_BLOCK = 256
VMEM_LIMIT = 56 * 1024 * 1024

COL_Q, COL_K, COL_V = 0, 8, 16
COL_GQ, COL_GK, COL_GV, COL_GZ = 24, 32, 40, 48
PROJ_MAIN_W = 11264
PROJ_TN = 1024

NT_DIMS = (((1,), (1,)), ((), ()))
TN_DIMS = (((0,), (0,)), ((), ()))


def _cparams(*sem):
    return pltpu.CompilerParams(dimension_semantics=sem, vmem_limit_bytes=VMEM_LIMIT)


def _const_spec(shape):
    nd = len(shape)
    return pl.BlockSpec(shape, lambda *_: (0,) * nd, pipeline_mode=pl.Buffered(1))


def _rmsnorm_rows(x, gain):
    ms = jnp.mean(x * x, axis=-1, keepdims=True)
    return x * lax.rsqrt(ms + NORM_EPS) * gain


def _sigmoid(x):
    return 1.0 / (1.0 + jnp.exp(-x))


def _split3(x):
    hi = x.astype(BF16)
    r1 = x - hi.astype(F32)
    mid = r1.astype(BF16)
    lo = (r1 - mid.astype(F32)).astype(BF16)
    return hi, mid, lo


def _dot(a, b):
    return jnp.dot(a, b, preferred_element_type=F32)


def _proj_kernel(x_ref, g_ref, w_ref, wab_ref, grp_ref, qg_ref, kg_ref, o_ref, ab_ref, hn_ref):
    j = pl.program_id(1)

    @pl.when(j == 0)
    def _():
        hb = _rmsnorm_rows(x_ref[...], g_ref[...]).astype(BF16)
        hn_ref[...] = hb
        ab_ref[...] = _dot(hb, wab_ref[...])

    acc = _dot(hn_ref[...], w_ref[...])

    def head_norm(gain_ref, scale):
        for c in range(PROJ_TN // MXU_DIM):
            a = acc[:, c * MXU_DIM:(c + 1) * MXU_DIM]
            sq = a * a
            hi = sq.astype(BF16)
            lo = (sq - hi.astype(F32)).astype(BF16)
            ss = _dot(hi, grp_ref[...]) + _dot(lo, grp_ref[...])
            y = a * lax.rsqrt(ss * (1.0 / DA_HEAD_DIM) + NORM_EPS) * (gain_ref[...] * scale)
            o_ref[:, c * MXU_DIM:(c + 1) * MXU_DIM] = y.astype(BF16)

    @pl.when(j == 0)
    def _():
        head_norm(qg_ref, DA_HEAD_DIM ** -0.5 * LOG2E)

    @pl.when(j == 1)
    def _():
        head_norm(kg_ref, 1.0)

    @pl.when((j >= 2) & (j < 7))
    def _():
        o_ref[...] = acc.astype(BF16)

    @pl.when(j >= 7)
    def _():
        o_ref[...] = _sigmoid(acc).astype(BF16)


def _stage_a(x2d, prm, tm):
    t = x2d.shape[0]
    return pl.pallas_call(
        _proj_kernel,
        grid=(t // tm, PROJ_MAIN_W // PROJ_TN),
        in_specs=[
            pl.BlockSpec((tm, D_MODEL), lambda i, j: (i, 0)),
            _const_spec((1, D_MODEL)),
            pl.BlockSpec((D_MODEL, PROJ_TN), lambda i, j: (0, j)),
            _const_spec((D_MODEL, LANES)),
            _const_spec((MXU_DIM, MXU_DIM)),
            _const_spec((1, MXU_DIM)),
            _const_spec((1, MXU_DIM)),
        ],
        out_specs=[
            pl.BlockSpec((tm, PROJ_TN), lambda i, j: (i, j)),
            pl.BlockSpec((tm, LANES), lambda i, j: (i, 0)),
        ],
        out_shape=[
            jax.ShapeDtypeStruct((t, PROJ_MAIN_W), BF16),
            jax.ShapeDtypeStruct((t, LANES), F32),
        ],
        scratch_shapes=[pltpu.VMEM((tm, D_MODEL), BF16)],
        compiler_params=_cparams("parallel", "arbitrary"),
        name="stage_a_proj",
    )(x2d, prm["norm_mix"], prm["w_main"], prm["w_ab"], prm["grp64"], prm["q_gain"], prm["k_gain"])


def _attn_kernel(slopes_ref, q_ref, k_ref, v_ref, lam_ref, subln_ref, o_ref,
                 q2_ref, m_ref, l_ref, acc_ref, *, tq, tk, seq):
    h = pl.program_id(1)
    qi = pl.program_id(2)
    q = q_ref[0]
    lane = lax.broadcasted_iota(jnp.int32, (tq, LANES), 1)
    zero = jnp.zeros_like(q)
    q2_ref[0:tq, :] = jnp.where(lane < DA_HEAD_DIM, q, zero)
    q2_ref[tq:2 * tq, :] = jnp.where(lane >= DA_HEAD_DIM, q, zero)
    m_ref[...] = jnp.full(m_ref.shape, -1e30, F32)
    l_ref[...] = jnp.zeros(l_ref.shape, F32)
    acc_ref[...] = jnp.zeros(acc_ref.shape, F32)

    slope = slopes_ref[h]
    rr = lax.broadcasted_iota(jnp.int32, (2 * tq, tk), 0)
    rr = jnp.where(rr >= tq, rr - tq, rr)
    cc = lax.broadcasted_iota(jnp.int32, (2 * tq, tk), 1)
    d0 = (rr - cc).astype(F32)
    q0 = qi * tq

    def body(ki, carry):
        k0 = pl.multiple_of(ki * tk, tk)
        k = k_ref[0, pl.ds(k0, tk), :]
        v = v_ref[0, pl.ds(k0, tk), :]
        s = lax.dot_general(q2_ref[...], k, NT_DIMS, preferred_element_type=F32)
        s = s - slope * jnp.abs(d0 + (q0 - k0).astype(F32))
        m_old = m_ref[...]
        m_new = jnp.maximum(m_old, jnp.max(s, axis=-1, keepdims=True))
        alpha = jnp.exp2(m_old - m_new)
        p = jnp.exp2(s - m_new)
        l_ref[...] = alpha * l_ref[...] + jnp.sum(p, axis=-1, keepdims=True)
        acc_ref[...] = alpha * acc_ref[...] + _dot(p.astype(BF16), v)
        m_ref[...] = m_new
        return carry

    lax.fori_loop(0, seq // tk, body, 0)

    o = acc_ref[...] * (1.0 / l_ref[...])
    lp = lam_ref[...]
    lam = (jnp.exp(jnp.sum(lp[0:1, :] * lp[1:2, :], axis=-1, keepdims=True))
           - jnp.exp(jnp.sum(lp[2:3, :] * lp[3:4, :], axis=-1, keepdims=True)) + LAM_INIT)
    d = o[0:tq, :] - lam * o[tq:2 * tq, :]
    y = _rmsnorm_rows(d, subln_ref[...]) * (1.0 - LAM_INIT)
    o_ref[0] = y.astype(BF16)


def _stage_b(proj3, prm, tq, tk):
    b, s, _ = proj3.shape
    kern = functools.partial(_attn_kernel, tq=tq, tk=tk, seq=s)
    return pl.pallas_call(
        kern,
        grid=(b, DA_HEADS, s // tq),
        in_specs=[
            pl.BlockSpec(memory_space=pltpu.SMEM),
            pl.BlockSpec((1, tq, LANES), lambda bi, h, qi: (bi, qi, COL_Q + h)),
            pl.BlockSpec((1, s, LANES), lambda bi, h, qi: (bi, 0, COL_K + h)),
            pl.BlockSpec((1, s, LANES), lambda bi, h, qi: (bi, 0, COL_V + h)),
            _const_spec((4, DA_HEAD_DIM)),
            _const_spec((1, LANES)),
        ],
        out_specs=pl.BlockSpec((1, tq, LANES), lambda bi, h, qi: (bi, qi, h)),
        out_shape=jax.ShapeDtypeStruct((b, s, DA_HEADS * LANES), BF16),
        scratch_shapes=[
            pltpu.VMEM((2 * tq, LANES), BF16),
            pltpu.VMEM((2 * tq, 1), F32),
            pltpu.VMEM((2 * tq, 1), F32),
            pltpu.VMEM((2 * tq, LANES), F32),
        ],
        compiler_params=_cparams("parallel", "parallel", "arbitrary"),
        name="stage_b_diff_attn",
    )(prm["slopes"], proj3, proj3, proj3, prm["lam_params"], prm["subln"])


def _softplus(x):
    return jnp.maximum(x, 0.0) + jnp.log1p(jnp.exp(-jnp.abs(x)))


def _gdn_gate_kernel(ab_ref, alog_ref, dtb_ref, ltri_ref, utri_ref, ones_ref, o_ref, ot_ref):
    x = ab_ref[...]
    g = -jnp.exp(alog_ref[...]) * _softplus(x + dtb_ref[...])
    parts = _split3(g)
    cf = sum(_dot(ltri_ref[...], p) for p in parts)
    cb = sum(_dot(utri_ref[...], p) for p in parts)
    tot = sum(_dot(ones_ref[...], p) for p in parts)
    lane = lax.broadcasted_iota(jnp.int32, x.shape, 1)
    out = jnp.where(lane < 8, cf,
                    jnp.where(lane < 16, cb,
                              jnp.where(lane < 32, _sigmoid(x), pltpu.roll(tot, 32, axis=1))))
    o_ref[...] = out
    ot_ref[0] = out.T


def _stage_c0(ab2d, prm):
    t = ab2d.shape[0]
    nb = t // GDN_BLOCK
    return pl.pallas_call(
        _gdn_gate_kernel,
        grid=(nb,),
        in_specs=[
            pl.BlockSpec((GDN_BLOCK, LANES), lambda i: (i, 0)),
            _const_spec((1, LANES)),
            _const_spec((1, LANES)),
            _const_spec((GDN_BLOCK, GDN_BLOCK)),
            _const_spec((GDN_BLOCK, GDN_BLOCK)),
            _const_spec((GDN_BLOCK, GDN_BLOCK)),
        ],
        out_specs=[
            pl.BlockSpec((GDN_BLOCK, LANES), lambda i: (i, 0)),
            pl.BlockSpec((1, LANES, GDN_BLOCK), lambda i: (i, 0, 0)),
        ],
        out_shape=[
            jax.ShapeDtypeStruct((t, LANES), F32),
            jax.ShapeDtypeStruct((nb, LANES, GDN_BLOCK), F32),
        ],
        compiler_params=_cparams("parallel"),
        name="stage_c0_gdn_gates",
    )(ab2d, prm["alog_row"], prm["dtb_row"], prm["ltri"], prm["utri"], prm["ones_bd"])


def _gdn_kernel(qp_ref, kp_ref, vp_ref, z_ref, wq_ref, wk_ref, wv_ref, gcb_ref, gct_ref, onorm_ref,
                o_ref, xp_ref, qn_ref, kn_ref, vn_ref, oacc_ref, st_ref, *, seq):
    h = pl.program_id(1)
    nblk = seq // GDN_BLOCK
    nch = GDN_BLOCK // GDN_CHUNK
    pad = 8

    def conv_phase(src_ref, w_ref, dst_ref, mode):
        xp_ref[0:pad, :] = jnp.zeros((pad, LANES), F32)
        xp_ref[pad + seq:pad + seq + pad, :] = jnp.zeros((pad, LANES), F32)

        def fill(i, c):
            r0 = pl.multiple_of(i * GDN_BLOCK, GDN_BLOCK)
            xp_ref[pl.ds(pad + r0, GDN_BLOCK), :] = src_ref[0, pl.ds(r0, GDN_BLOCK), :].astype(F32)
            return c

        lax.fori_loop(0, nblk, fill, 0)

        def conv(i, c):
            r0 = pl.multiple_of(i * GDN_BLOCK, GDN_BLOCK)
            acc = jnp.zeros((GDN_BLOCK, LANES), F32)
            for t in range(CONV_K):
                acc = acc + w_ref[t:t + 1, :] * xp_ref[pl.ds(r0 + pad + t - (CONV_K - 1) // 2, GDN_BLOCK), :]
            y = acc * _sigmoid(acc)
            if mode != "v":
                y = y * lax.rsqrt(jnp.sum(y * y, axis=-1, keepdims=True) + NORM_EPS)
            if mode == "q":
                y = y * (GDN_DIM ** -0.5)
            dst_ref[pl.ds(r0, GDN_BLOCK), :] = y
            return c

        lax.fori_loop(0, nblk, conv, 0)

    conv_phase(qp_ref, wq_ref, qn_ref, "q")
    conv_phase(kp_ref, wk_ref, kn_ref, "k")
    conv_phase(vp_ref, wv_ref, vn_ref, "v")

    oacc_ref[...] = jnp.zeros(oacc_ref.shape, F32)
    st_ref[...] = jnp.zeros(st_ref.shape, F32)

    ri = lax.broadcasted_iota(jnp.int32, (GDN_BLOCK, GDN_BLOCK), 0)
    ci = lax.broadcasted_iota(jnp.int32, (GDN_BLOCK, GDN_BLOCK), 1)
    same = (ri // GDN_CHUNK) == (ci // GDN_CHUNK)
    eye = (ri == ci).astype(F32)
    lane = lax.broadcasted_iota(jnp.int32, (GDN_BLOCK, LANES), 1)

    def pick(gblk, col):
        return jnp.sum(jnp.where(lane == col, gblk, 0.0), axis=-1, keepdims=True)

    def block_step(blk, direction):
        r0 = pl.multiple_of(blk * GDN_BLOCK, GDN_BLOCK)
        q = qn_ref[pl.ds(r0, GDN_BLOCK), :]
        k = kn_ref[pl.ds(r0, GDN_BLOCK), :]
        v = vn_ref[pl.ds(r0, GDN_BLOCK), :]
        gblk = gcb_ref[0, pl.ds(r0, GDN_BLOCK), :]
        col = direction * GDN_HEADS + h
        gcol = pick(gblk, col)
        bcol = pick(gblk, 16 + col)
        tcol = pick(gblk, 32 + col)
        grow = gct_ref[0, blk, pl.ds(col, 1), :]
        if direction == 0:
            incl, strict = same & (ci <= ri), same & (ci < ri)
        else:
            incl, strict = same & (ci >= ri), same & (ci > ri)
        decay = jnp.where(incl, jnp.exp(jnp.where(incl, gcol - grow, 0.0)), 0.0)

        kbf = k.astype(BF16)
        kb = k * bcol
        kq = lax.dot_general(jnp.concatenate([kb, q], axis=0).astype(BF16), kbf, NT_DIMS,
                             preferred_element_type=F32)
        a = jnp.where(strict, kq[0:GDN_BLOCK, :] * decay, 0.0)
        qkm = kq[GDN_BLOCK:, :] * decay

        t_inv = eye - a
        pw = a.astype(BF16)
        n_sq = int(math.log2(GDN_CHUNK)) - 1
        for _ in range(n_sq):
            pw32 = _dot(pw, pw)
            pw = pw32.astype(BF16)
            t_inv = t_inv + _dot(t_inv.astype(BF16), pw)

        eg = jnp.exp(gcol)
        rhs = jnp.concatenate([v * bcol, kb * eg], axis=1).astype(BF16)
        sol = _dot(t_inv.astype(BF16), rhs)
        solb = sol.astype(BF16)
        intra = _dot(qkm.astype(BF16), solb)
        o_intra = intra[:, 0:GDN_DIM]
        qt = q * eg - intra[:, GDN_DIM:]
        kdec = (k * jnp.exp(tcol - gcol)).astype(BF16)
        egl = jnp.exp(tcol)

        order = range(nch) if direction == 0 else range(nch - 1, -1, -1)
        for c in order:
            rows = slice(c * GDN_CHUNK, (c + 1) * GDN_CHUNK)
            bk = lax.dot_general(kdec[rows, :], solb[rows, :], TN_DIMS, preferred_element_type=F32)
            b_c, kw_c = bk[:, 0:GDN_DIM], bk[:, GDN_DIM:]
            state = st_ref[direction]
            x = jnp.concatenate([kw_c, qt[rows, :]], axis=0).astype(BF16)
            y = _dot(x, state.astype(BF16))
            o_c = y[GDN_DIM:, :] + o_intra[rows, :]
            orow = pl.ds(r0 + c * GDN_CHUNK, GDN_CHUNK)
            oacc_ref[orow, :] = oacc_ref[orow, :] + o_c
            st_ref[direction] = state * egl[c * GDN_CHUNK:c * GDN_CHUNK + 1, :] + b_c - y[0:GDN_DIM, :]

    def main(i, c):
        block_step(i, 0)
        block_step(nblk - 1 - i, 1)
        return c

    lax.fori_loop(0, nblk, main, 0)

    def fin(i, c):
        r0 = pl.multiple_of(i * GDN_BLOCK, GDN_BLOCK)
        o = oacc_ref[pl.ds(r0, GDN_BLOCK), :]
        z = z_ref[0, pl.ds(r0, GDN_BLOCK), :].astype(F32)
        y = _rmsnorm_rows(o, onorm_ref[...]) * (z * _sigmoid(z))
        o_ref[0, pl.ds(r0, GDN_BLOCK), :] = y.astype(BF16)
        return c

    lax.fori_loop(0, nblk, fin, 0)


def _stage_c(proj3, gcb3, gct4, prm):
    b, s, _ = proj3.shape
    nblk = s // GDN_BLOCK

    def once(shape, imap):
        return pl.BlockSpec(shape, imap, pipeline_mode=pl.Buffered(1))

    kern = functools.partial(_gdn_kernel, seq=s)
    return pl.pallas_call(
        kern,
        grid=(b, GDN_HEADS),
        in_specs=[
            once((1, s, LANES), lambda bi, h: (bi, 0, COL_GQ + h)),
            once((1, s, LANES), lambda bi, h: (bi, 0, COL_GK + h)),
            once((1, s, LANES), lambda bi, h: (bi, 0, COL_GV + h)),
            once((1, s, LANES), lambda bi, h: (bi, 0, COL_GZ + h)),
            pl.BlockSpec((8, LANES), lambda bi, h: (0, h)),
            pl.BlockSpec((8, LANES), lambda bi, h: (0, GDN_HEADS + h)),
            pl.BlockSpec((8, LANES), lambda bi, h: (0, 2 * GDN_HEADS + h)),
            once((1, s, LANES), lambda bi, h: (bi, 0, 0)),
            once((1, nblk, LANES, GDN_BLOCK), lambda bi, h: (bi, 0, 0, 0)),
            _const_spec((1, LANES)),
        ],
        out_specs=pl.BlockSpec((1, s, LANES), lambda bi, h: (bi, 0, h)),
        out_shape=jax.ShapeDtypeStruct((b, s, GDN_HEADS * GDN_DIM), BF16),
        scratch_shapes=[
            pltpu.VMEM((s + 16, LANES), F32),
            pltpu.VMEM((s, LANES), F32),
            pltpu.VMEM((s, LANES), F32),
            pltpu.VMEM((s, LANES), F32),
            pltpu.VMEM((s, LANES), F32),
            pltpu.VMEM((2, GDN_DIM, GDN_DIM), F32),
        ],
        compiler_params=_cparams("parallel", "arbitrary"),
        name="stage_c_gdn",
    )(proj3, proj3, proj3, proj3, prm["conv_w"], prm["conv_w"], prm["conv_w"], gcb3, gct4, prm["out_norm"])


def _merge_kernel(ya_ref, yb_ref, ga0_ref, ga1_ref, gb0_ref, gb1_ref, x_ref, wa_ref, wb_ref, wo_ref, o_ref):
    half = D_MODEL // 2
    pa = _dot(ya_ref[...], wa_ref[...])
    pb = _dot(yb_ref[...], wb_ref[...])
    m0 = ga0_ref[...].astype(F32) * pa[:, 0:half] + gb0_ref[...].astype(F32) * pb[:, 0:half]
    m1 = ga1_ref[...].astype(F32) * pa[:, half:] + gb1_ref[...].astype(F32) * pb[:, half:]
    mixed = jnp.concatenate([m0, m1], axis=1).astype(BF16)
    o_ref[...] = x_ref[...] + _dot(mixed, wo_ref[...])


def _stage_d(ya2d, yb2d, proj2d, x2d, prm, tm):
    t = x2d.shape[0]
    half = D_MODEL // 2
    gate0 = (COL_GZ + GDN_HEADS) * LANES // half
    row = lambda c: pl.BlockSpec((tm, half), lambda i: (i, c))
    return pl.pallas_call(
        _merge_kernel,
        grid=(t // tm,),
        in_specs=[
            row(0), row(0), row(gate0), row(gate0 + 1), row(gate0 + 2), row(gate0 + 3),
            pl.BlockSpec((tm, D_MODEL), lambda i: (i, 0)),
            _const_spec((half, D_MODEL)),
            _const_spec((half, D_MODEL)),
            _const_spec((D_MODEL, D_MODEL)),
        ],
        out_specs=pl.BlockSpec((tm, D_MODEL), lambda i: (i, 0)),
        out_shape=jax.ShapeDtypeStruct((t, D_MODEL), F32),
        compiler_params=_cparams("parallel"),
        name="stage_d_merge",
    )(ya2d, yb2d, proj2d, proj2d, proj2d, proj2d, x2d, prm["w_branch_a"], prm["w_branch_b"], prm["w_out"])


def _ffn_kernel(x_ref, g_ref, wu_ref, wg_ref, wo_ref, o_ref, hn_ref):
    f = pl.program_id(1)

    @pl.when(f == 0)
    def _():
        x = x_ref[...]
        hn_ref[...] = _rmsnorm_rows(x, g_ref[...]).astype(BF16)
        o_ref[...] = x

    hn = hn_ref[...]
    up = _dot(hn, wu_ref[...])
    gt = _dot(hn, wg_ref[...])
    act = (gt * _sigmoid(gt) * up).astype(BF16)
    o_ref[...] += _dot(act, wo_ref[...])


def _stage_e(x2d, prm, tm, tf):
    t = x2d.shape[0]
    nf = FFN_HIDDEN // tf
    return pl.pallas_call(
        _ffn_kernel,
        grid=(t // tm, nf),
        in_specs=[
            pl.BlockSpec((tm, D_MODEL), lambda i, f: (i, 0)),
            _const_spec((1, D_MODEL)),
            pl.BlockSpec((D_MODEL, tf), lambda i, f: (0, f)),
            pl.BlockSpec((D_MODEL, tf), lambda i, f: (0, nf + f)),
            pl.BlockSpec((tf, D_MODEL), lambda i, f: (f, 0)),
        ],
        out_specs=pl.BlockSpec((tm, D_MODEL), lambda i, f: (i, 0)),
        out_shape=jax.ShapeDtypeStruct((t, D_MODEL), F32),
        scratch_shapes=[pltpu.VMEM((tm, D_MODEL), BF16)],
        compiler_params=_cparams("parallel", "arbitrary"),
        name="stage_e_ffn",
    )(x2d, prm["norm_ffn"], prm["w_ffn_in"], prm["w_ffn_in"], prm["w_ffn_out"])


def _ple_kernel(x_ref, g_ref, wg_ref, p_ref, wp_ref, o_ref, hn_ref, *, tn):
    j = pl.program_id(1)

    @pl.when(j == 0)
    def _():
        hn_ref[...] = _rmsnorm_rows(x_ref[...], g_ref[...]).astype(BF16)

    gate = _sigmoid(_dot(hn_ref[...], wg_ref[...]))
    emb = _dot(p_ref[...].astype(BF16), wp_ref[...])
    c0 = pl.multiple_of(j * tn, tn)
    o_ref[...] = x_ref[:, pl.ds(c0, tn)] + gate * emb


def _stage_f(x2d, p2d, prm, tm, tn):
    t = x2d.shape[0]
    kern = functools.partial(_ple_kernel, tn=tn)
    return pl.pallas_call(
        kern,
        grid=(t // tm, D_MODEL // tn),
        in_specs=[
            pl.BlockSpec((tm, D_MODEL), lambda i, j: (i, 0)),
            _const_spec((1, D_MODEL)),
            pl.BlockSpec((D_MODEL, tn), lambda i, j: (0, j)),
            pl.BlockSpec((tm, PLE_DIM), lambda i, j: (i, 0)),
            pl.BlockSpec((PLE_DIM, tn), lambda i, j: (0, j)),
        ],
        out_specs=pl.BlockSpec((tm, tn), lambda i, j: (i, j)),
        out_shape=jax.ShapeDtypeStruct((t, D_MODEL), F32),
        scratch_shapes=[pltpu.VMEM((tm, D_MODEL), BF16)],
        compiler_params=_cparams("parallel", "arbitrary"),
        name="stage_f_ple",
    )(x2d, prm["norm_ple"], prm["w_ple_gate"], p2d, prm["w_ple_proj"])


def _block_diag_mask(n, blk, kind):
    r = jnp.arange(n)[:, None]
    c = jnp.arange(n)[None, :]
    same = (r // blk) == (c // blk)
    if kind == "lower":
        same = same & (c <= r)
    elif kind == "upper":
        same = same & (c >= r)
    return same.astype(BF16)


def _prepare_params(norm_mix, w_in, da_q_norm, da_k_norm, lambda_q1, lambda_k1, lambda_q2, lambda_k2,
                    da_subln, gdn_conv, gdn_a_log, gdn_dt_bias, gdn_out_norm, w_branch_a, w_branch_b,
                    w_out, norm_ffn, w_ffn_in, w_ffn_out, norm_ple, w_ple_gate, w_ple_proj):
    n_fixed = COL_GZ * LANES + GDN_HEADS * GDN_DIM
    n_ab = 4 * GDN_HEADS
    row = lambda v: v.reshape(1, -1).astype(F32)
    return {
        "norm_mix": row(norm_mix),
        "w_main": jnp.concatenate([w_in[:, :n_fixed], w_in[:, n_fixed + n_ab:]], axis=1).astype(BF16),
        "w_ab": jnp.pad(w_in[:, n_fixed:n_fixed + n_ab], ((0, 0), (0, LANES - n_ab))).astype(BF16),
        "grp64": _block_diag_mask(MXU_DIM, DA_HEAD_DIM, "full"),
        "q_gain": row(jnp.tile(da_q_norm, MXU_DIM // DA_HEAD_DIM)),
        "k_gain": row(jnp.tile(da_k_norm, MXU_DIM // DA_HEAD_DIM)),
        "slopes": (2.0 ** (-8.0 * jnp.arange(1, DA_HEADS + 1, dtype=F32) / DA_HEADS)) * LOG2E,
        "lam_params": jnp.stack([lambda_q1, lambda_k1, lambda_q2, lambda_k2]).astype(F32),
        "subln": row(da_subln),
        "conv_w": jnp.pad(gdn_conv.astype(F32), ((0, 8 - CONV_K), (0, 0))),
        "alog_row": jnp.pad(gdn_a_log.reshape(1, -1).astype(F32), ((0, 0), (0, LANES - 2 * GDN_HEADS))),
        "dtb_row": jnp.pad(gdn_dt_bias.reshape(1, -1).astype(F32), ((0, 0), (0, LANES - 2 * GDN_HEADS))),
        "ltri": _block_diag_mask(GDN_BLOCK, GDN_CHUNK, "lower"),
        "utri": _block_diag_mask(GDN_BLOCK, GDN_CHUNK, "upper"),
        "ones_bd": _block_diag_mask(GDN_BLOCK, GDN_CHUNK, "full"),
        "out_norm": row(gdn_out_norm),
        "w_branch_a": w_branch_a.astype(BF16),
        "w_branch_b": w_branch_b.astype(BF16),
        "w_out": w_out.astype(BF16),
        "norm_ffn": row(norm_ffn),
        "w_ffn_in": w_ffn_in.astype(BF16),
        "w_ffn_out": w_ffn_out.astype(BF16),
        "norm_ple": row(norm_ple),
        "w_ple_gate": w_ple_gate.astype(BF16),
        "w_ple_proj": w_ple_proj.astype(BF16),
    }


def _tile(n, pref):
    return pref if n % pref == 0 else n


def _encoder_layer(x, p, prm):
    b, s, _ = x.shape
    t = b * s
    x2d = x.reshape(t, D_MODEL)
    proj, ab = _stage_a(x2d, prm, _tile(t, 1024))
    proj3 = proj.reshape(b, s, PROJ_MAIN_W)
    ya = _stage_b(proj3, prm, _tile(s, 256), _tile(s, 512))
    gcb, gct = _stage_c0(ab, prm)
    yb = _stage_c(proj3, gcb.reshape(b, s, LANES), gct.reshape(b, s // GDN_BLOCK, LANES, GDN_BLOCK), prm)
    x1 = _stage_d(ya.reshape(t, -1), yb.reshape(t, -1), proj, x2d, prm, _tile(t, 256))
    x2 = _stage_e(x1, prm, _tile(t, 512), 512)
    x3 = _stage_f(x2, p.reshape(t, PLE_DIM), prm, _tile(t, 512), 1024)
    return x3.reshape(b, s, D_MODEL)


def kernel(x_prompt, x_sample, p_prompt, p_sample, norm_mix, w_in, da_q_norm, da_k_norm, lambda_q1, lambda_k1, lambda_q2, lambda_k2, da_subln, gdn_conv, gdn_a_log, gdn_dt_bias, gdn_out_norm, w_branch_a, w_branch_b, w_out, norm_ffn, w_ffn_in, w_ffn_out, norm_ple, w_ple_gate, w_ple_proj):
    layer_params = (norm_mix, w_in, da_q_norm, da_k_norm, lambda_q1, lambda_k1, lambda_q2, lambda_k2,
                    da_subln, gdn_conv, gdn_a_log, gdn_dt_bias, gdn_out_norm, w_branch_a, w_branch_b,
                    w_out, norm_ffn, w_ffn_in, w_ffn_out, norm_ple, w_ple_gate, w_ple_proj)
    depth = norm_mix.shape[0]
    assert depth == 1, "LAM_INIT is the depth-0 value"
    y_prompt, y_sample = x_prompt, x_sample
    for layer in range(depth):
        prm = _prepare_params(*(w[layer] for w in layer_params))
        y_prompt = _encoder_layer(y_prompt, p_prompt[layer], prm)
        y_sample = _encoder_layer(y_sample, p_sample[layer], prm)
    return (y_prompt, y_sample)
```

```python
import functools
import math

import jax
import jax.numpy as jnp
from jax import lax
from jax.experimental import pallas as pl
from jax.experimental.pallas import tpu as pltpu

F32 = jnp.float32
BF16 = jnp.bfloat16

D_MODEL = 2048
DA_HEADS = 8
DA_HEAD_DIM = 64
GDN_HEADS = 8
GDN_DIM = 128
GDN_CHUNK = 64
CONV_K = 5
FFN_HIDDEN = 5632
PLE_DIM = 256
NORM_EPS = 1e-6
LAM_INIT = 0.8 - 0.6 * math.exp(-0.3 * 0)
LOG2E = math.log2(math.e)

LANES = 128
MXU_DIM = 256
GDN_BLOCK = 256
VMEM_LIMIT = 56 * 1024 * 1024

COL_Q, COL_K, COL_V = 0, 8, 16
COL_GQ, COL_GK, COL_GV, COL_GZ = 24, 32, 40, 48
PROJ_MAIN_W = 11264
PROJ_TN = 1024

NT_DIMS = (((1,), (1,)), ((), ()))
TN_DIMS = (((0,), (0,)), ((), ()))


def _cparams(*sem):
    return pltpu.CompilerParams(dimension_semantics=sem, vmem_limit_bytes=VMEM_LIMIT)


def _const_spec(shape):
    nd = len(shape)
    return pl.BlockSpec(shape, lambda *_: (0,) * nd, pipeline_mode=pl.Buffered(1))


def _rmsnorm_rows(x, gain):
    ms = jnp.mean(x * x, axis=-1, keepdims=True)
    return x * lax.rsqrt(ms + NORM_EPS) * gain


def _sigmoid(x):
    return 1.0 / (1.0 + jnp.exp(-x))


def _split3(x):
    hi = x.astype(BF16)
    r1 = x - hi.astype(F32)
    mid = r1.astype(BF16)
    lo = (r1 - mid.astype(F32)).astype(BF16)
    return hi, mid, lo


def _dot(a, b):
    return jnp.dot(a, b, preferred_element_type=F32)


def _proj_kernel(x_ref, g_ref, w_ref, wab_ref, grp_ref, qg_ref, kg_ref, o_ref, ab_ref, hn_ref):
    j = pl.program_id(1)

    @pl.when(j == 0)
    def _():
        hb = _rmsnorm_rows(x_ref[...], g_ref[...]).astype(BF16)
        hn_ref[...] = hb
        ab_ref[...] = _dot(hb, wab_ref[...])

    acc = _dot(hn_ref[...], w_ref[...])

    def head_norm(gain_ref, scale):
        for c in range(PROJ_TN // MXU_DIM):
            a = acc[:, c * MXU_DIM:(c + 1) * MXU_DIM]
            sq = a * a
            hi = sq.astype(BF16)
            lo = (sq - hi.astype(F32)).astype(BF16)
            ss = _dot(hi, grp_ref[...]) + _dot(lo, grp_ref[...])
            y = a * lax.rsqrt(ss * (1.0 / DA_HEAD_DIM) + NORM_EPS) * (gain_ref[...] * scale)
            o_ref[:, c * MXU_DIM:(c + 1) * MXU_DIM] = y.astype(BF16)

    @pl.when(j == 0)
    def _():
        head_norm(qg_ref, DA_HEAD_DIM ** -0.5 * LOG2E)

    @pl.when(j == 1)
    def _():
        head_norm(kg_ref, 1.0)

    @pl.when((j >= 2) & (j < 7))
    def _():
        o_ref[...] = acc.astype(BF16)

    @pl.when(j >= 7)
    def _():
        o_ref[...] = _sigmoid(acc).astype(BF16)


def _stage_a(x2d, prm, tm):
    t = x2d.shape[0]
    return pl.pallas_call(
        _proj_kernel,
        grid=(t // tm, PROJ_MAIN_W // PROJ_TN),
        in_specs=[
            pl.BlockSpec((tm, D_MODEL), lambda i, j: (i, 0)),
            _const_spec((1, D_MODEL)),
            pl.BlockSpec((D_MODEL, PROJ_TN), lambda i, j: (0, j)),
            _const_spec((D_MODEL, LANES)),
            _const_spec((MXU_DIM, MXU_DIM)),
            _const_spec((1, MXU_DIM)),
            _const_spec((1, MXU_DIM)),
        ],
        out_specs=[
            pl.BlockSpec((tm, PROJ_TN), lambda i, j: (i, j)),
            pl.BlockSpec((tm, LANES), lambda i, j: (i, 0)),
        ],
        out_shape=[
            jax.ShapeDtypeStruct((t, PROJ_MAIN_W), BF16),
            jax.ShapeDtypeStruct((t, LANES), F32),
        ],
        scratch_shapes=[pltpu.VMEM((tm, D_MODEL), BF16)],
        compiler_params=_cparams("parallel", "arbitrary"),
        name="stage_a_proj",
    )(x2d, prm["norm_mix"], prm["w_main"], prm["w_ab"], prm["grp64"], prm["q_gain"], prm["k_gain"])


def _attn_kernel(slopes_ref, q_ref, k_ref, v_ref, lam_ref, subln_ref, o_ref,
                 kx_ref, vx_ref, qx_ref, sa_ref, sb_ref, m_ref, acc_ref, *, tq, tk, seq):
    h = pl.program_id(1)
    qi = pl.program_id(2)
    slope = slopes_ref[h]

    def split3_f32(x):
        return tuple(p.astype(F32) for p in _split3(x))

    def ext_lanes(shape, pieces, unit):
        lane = lax.broadcasted_iota(jnp.int32, shape, 1)
        hi, mid, lo = pieces
        return jnp.where(lane == 0, hi, jnp.where(lane == 1, mid, jnp.where(lane == 2, lo,
                         jnp.where(lane < 6, unit, 0.0)))).astype(BF16)

    @pl.when(qi == 0)
    def _():
        def fill(i, c):
            r0 = pl.multiple_of(i * tk, tk)
            pos = (r0 + lax.broadcasted_iota(jnp.int32, (tk, LANES), 0)).astype(F32)
            hi, mid, lo = split3_f32(slope * pos)
            lane = lax.broadcasted_iota(jnp.int32, (tk, LANES), 1)
            kext = jnp.where(lane < 3, 1.0, jnp.where(lane == 3, hi, jnp.where(lane == 4, mid,
                             jnp.where(lane == 5, lo, 0.0)))).astype(BF16)
            kx_ref[pl.ds(r0, tk), 0:LANES] = k_ref[0, pl.ds(r0, tk), :]
            kx_ref[pl.ds(r0, tk), LANES:2 * LANES] = kext
            vx_ref[pl.ds(r0, tk), 0:LANES] = v_ref[0, pl.ds(r0, tk), :]
            vx_ref[pl.ds(r0, tk), LANES:2 * LANES] = jnp.ones((tk, LANES), BF16)
            return c

        lax.fori_loop(0, seq // tk, fill, 0)

    q0 = qi * tq
    q = q_ref[0]
    lane = lax.broadcasted_iota(jnp.int32, (tq, LANES), 1)
    qm0 = jnp.where(lane < DA_HEAD_DIM, q, jnp.zeros_like(q))
    qm1 = jnp.where(lane >= DA_HEAD_DIM, q, jnp.zeros_like(q))
    pos = (q0 + lax.broadcasted_iota(jnp.int32, (tq, LANES), 0)).astype(F32)
    hi, mid, lo = split3_f32(slope * pos)
    ext_l = ext_lanes((tq, LANES), (-hi, -mid, -lo), 1.0)
    ext_r = ext_lanes((tq, LANES), (hi, mid, lo), -1.0)
    for side, ext in ((0, ext_l), (1, ext_r)):
        qx_ref[side, 0:tq, 0:LANES] = qm0
        qx_ref[side, tq:2 * tq, 0:LANES] = qm1
        qx_ref[side, 0:tq, LANES:2 * LANES] = ext
        qx_ref[side, tq:2 * tq, LANES:2 * LANES] = ext
    m_ref[...] = jnp.full(m_ref.shape, -1e30, F32)
    acc_ref[...] = jnp.zeros(acc_ref.shape, F32)

    nk = seq // tk
    n_left = q0 // tk

    def kblock(ki):
        return kx_ref[pl.ds(pl.multiple_of(ki * tk, tk), tk), :]

    def scores(ki):
        side = jnp.where(ki >= n_left, 1, 0)
        return lax.dot_general(qx_ref[side], kblock(ki), NT_DIMS, preferred_element_type=F32)

    def consume(s, ki):
        m_old = m_ref[...]
        m_new = jnp.maximum(m_old, jnp.max(s, axis=-1, keepdims=True))
        alpha = jnp.exp2(m_old - m_new)
        p = jnp.exp2(s - jnp.concatenate([m_new] * (tk // LANES), axis=1))
        vx = vx_ref[pl.ds(pl.multiple_of(ki * tk, tk), tk), :]
        pv = _dot(p.astype(BF16), vx)
        acc_ref[...] = jnp.concatenate([alpha, alpha], axis=1) * acc_ref[...] + pv
        m_ref[...] = m_new

    kd = kblock(n_left)
    consume(jnp.minimum(lax.dot_general(qx_ref[0], kd, NT_DIMS, preferred_element_type=F32),
                        lax.dot_general(qx_ref[1], kd, NT_DIMS, preferred_element_type=F32)), n_left)
    n_rest = nk - 1
    if n_rest > 0:
        assert n_rest % 2 == 1, "the two-slot pipeline below consumes the off-diagonal blocks in pairs plus one"
        nth = lambda j: jnp.where(j >= n_left, j + 1, j)

        sa_ref[...] = scores(nth(0))

        def body(t, carry):
            j = 2 * t
            sb_ref[...] = scores(nth(j + 1))
            consume(sa_ref[...], nth(j))
            sa_ref[...] = scores(nth(j + 2))
            consume(sb_ref[...], nth(j + 1))
            return carry

        lax.fori_loop(0, (n_rest - 1) // 2, body, 0)
        consume(sa_ref[...], nth(n_rest - 1))

    acc = acc_ref[...]
    o = acc[:, 0:LANES] * (1.0 / acc[:, LANES:2 * LANES])
    lp = lam_ref[...]
    lam = (jnp.exp(jnp.sum(lp[0:1, :] * lp[1:2, :], axis=-1, keepdims=True))
           - jnp.exp(jnp.sum(lp[2:3, :] * lp[3:4, :], axis=-1, keepdims=True)) + LAM_INIT)
    d = o[0:tq, :] - lam * o[tq:2 * tq, :]
    y = _rmsnorm_rows(d, subln_ref[...]) * (1.0 - LAM_INIT)
    o_ref[0] = y.astype(BF16)


def _stage_b(proj3, prm, tq, tk):
    b, s, _ = proj3.shape
    assert tk % tq == 0 and s % tk == 0, "each query block must sit inside one key block"
    kern = functools.partial(_attn_kernel, tq=tq, tk=tk, seq=s)
    return pl.pallas_call(
        kern,
        grid=(b, DA_HEADS, s // tq),
        in_specs=[
            pl.BlockSpec(memory_space=pltpu.SMEM),
            pl.BlockSpec((1, tq, LANES), lambda bi, h, qi: (bi, qi, COL_Q + h)),
            pl.BlockSpec((1, s, LANES), lambda bi, h, qi: (bi, 0, COL_K + h)),
            pl.BlockSpec((1, s, LANES), lambda bi, h, qi: (bi, 0, COL_V + h)),
            _const_spec((4, DA_HEAD_DIM)),
            _const_spec((1, LANES)),
        ],
        out_specs=pl.BlockSpec((1, tq, LANES), lambda bi, h, qi: (bi, qi, h)),
        out_shape=jax.ShapeDtypeStruct((b, s, DA_HEADS * LANES), BF16),
        scratch_shapes=[
            pltpu.VMEM((s, 2 * LANES), BF16),
            pltpu.VMEM((s, 2 * LANES), BF16),
            pltpu.VMEM((2, 2 * tq, 2 * LANES), BF16),
            pltpu.VMEM((2 * tq, tk), F32),
            pltpu.VMEM((2 * tq, tk), F32),
            pltpu.VMEM((2 * tq, LANES), F32),
            pltpu.VMEM((2 * tq, 2 * LANES), F32),
        ],
        compiler_params=_cparams("parallel", "parallel", "arbitrary"),
        name="stage_b_diff_attn",
    )(prm["slopes"], proj3, proj3, proj3, prm["lam_params"], prm["subln"])


def _softplus(x):
    return jnp.maximum(x, 0.0) + jnp.log1p(jnp.exp(-jnp.abs(x)))


def _gdn_gate_kernel(ab_ref, alog_ref, dtb_ref, ltri_ref, utri_ref, ones_ref, o_ref, ot_ref):
    x = ab_ref[...]
    g = -jnp.exp(alog_ref[...]) * _softplus(x + dtb_ref[...])
    parts = _split3(g)
    cf = sum(_dot(ltri_ref[...], p) for p in parts)
    cb = sum(_dot(utri_ref[...], p) for p in parts)
    tot = sum(_dot(ones_ref[...], p) for p in parts)
    lane = lax.broadcasted_iota(jnp.int32, x.shape, 1)
    out = jnp.where(lane < 8, cf,
                    jnp.where(lane < 16, cb,
                              jnp.where(lane < 32, _sigmoid(x), pltpu.roll(tot, 32, axis=1))))
    o_ref[...] = out
    ot_ref[0] = out.T


def _stage_c0(ab2d, prm):
    t = ab2d.shape[0]
    nb = t // GDN_BLOCK
    return pl.pallas_call(
        _gdn_gate_kernel,
        grid=(nb,),
        in_specs=[
            pl.BlockSpec((GDN_BLOCK, LANES), lambda i: (i, 0)),
            _const_spec((1, LANES)),
            _const_spec((1, LANES)),
            _const_spec((GDN_BLOCK, GDN_BLOCK)),
            _const_spec((GDN_BLOCK, GDN_BLOCK)),
            _const_spec((GDN_BLOCK, GDN_BLOCK)),
        ],
        out_specs=[
            pl.BlockSpec((GDN_BLOCK, LANES), lambda i: (i, 0)),
            pl.BlockSpec((1, LANES, GDN_BLOCK), lambda i: (i, 0, 0)),
        ],
        out_shape=[
            jax.ShapeDtypeStruct((t, LANES), F32),
            jax.ShapeDtypeStruct((nb, LANES, GDN_BLOCK), F32),
        ],
        compiler_params=_cparams("parallel"),
        name="stage_c0_gdn_gates",
    )(ab2d, prm["alog_row"], prm["dtb_row"], prm["ltri"], prm["utri"], prm["ones_bd"])


def _gdn_kernel(qp_ref, kp_ref, vp_ref, z_ref, wq_ref, wk_ref, wv_ref, gcb_ref, gct_ref, onorm_ref,
                o_ref, xp_ref, qn_ref, kn_ref, vn_ref, of_ref, st_ref, xs_ref, bs_ref, oi_ref, eg_ref, *, seq):
    h = pl.program_id(1)
    nblk = seq // GDN_BLOCK
    nch = GDN_BLOCK // GDN_CHUNK
    pad = 8

    def conv_phase(src_ref, w_ref, dst_ref, mode):
        xp_ref[0:pad, :] = jnp.zeros((pad, LANES), F32)
        xp_ref[pad + seq:pad + seq + pad, :] = jnp.zeros((pad, LANES), F32)

        def fill(i, c):
            r0 = pl.multiple_of(i * GDN_BLOCK, GDN_BLOCK)
            xp_ref[pl.ds(pad + r0, GDN_BLOCK), :] = src_ref[0, pl.ds(r0, GDN_BLOCK), :].astype(F32)
            return c

        lax.fori_loop(0, nblk, fill, 0)

        def conv(i, c):
            r0 = pl.multiple_of(i * GDN_BLOCK, GDN_BLOCK)
            acc = jnp.zeros((GDN_BLOCK, LANES), F32)
            for t in range(CONV_K):
                acc = acc + w_ref[t:t + 1, :] * xp_ref[pl.ds(r0 + pad + t - (CONV_K - 1) // 2, GDN_BLOCK), :]
            y = acc * _sigmoid(acc)
            if mode != "v":
                y = y * lax.rsqrt(jnp.sum(y * y, axis=-1, keepdims=True) + NORM_EPS)
            if mode == "q":
                y = y * (GDN_DIM ** -0.5)
            dst_ref[pl.ds(r0, GDN_BLOCK), :] = y
            return c

        lax.fori_loop(0, nblk, conv, 0)

    conv_phase(qp_ref, wq_ref, qn_ref, "q")
    conv_phase(kp_ref, wk_ref, kn_ref, "k")
    conv_phase(vp_ref, wv_ref, vn_ref, "v")

    st_ref[...] = jnp.zeros(st_ref.shape, F32)

    ri = lax.broadcasted_iota(jnp.int32, (GDN_BLOCK, GDN_BLOCK), 0)
    ci = lax.broadcasted_iota(jnp.int32, (GDN_BLOCK, GDN_BLOCK), 1)
    same = (ri // GDN_CHUNK) == (ci // GDN_CHUNK)
    eye = (ri == ci).astype(F32)
    lane = lax.broadcasted_iota(jnp.int32, (GDN_BLOCK, LANES), 1)

    def pick(gblk, col):
        return jnp.sum(jnp.where(lane == col, gblk, 0.0), axis=-1, keepdims=True)

    def prep(blk, direction, slot):
        r0 = pl.multiple_of(blk * GDN_BLOCK, GDN_BLOCK)
        q = qn_ref[pl.ds(r0, GDN_BLOCK), :]
        k = kn_ref[pl.ds(r0, GDN_BLOCK), :]
        v = vn_ref[pl.ds(r0, GDN_BLOCK), :]
        gblk = gcb_ref[0, pl.ds(r0, GDN_BLOCK), :]
        col = direction * GDN_HEADS + h
        gcol = pick(gblk, col)
        bcol = pick(gblk, 16 + col)
        tcol = pick(gblk, 32 + col)
        grow = gct_ref[0, blk, pl.ds(col, 1), :]
        if direction == 0:
            incl, strict = same & (ci <= ri), same & (ci < ri)
        else:
            incl, strict = same & (ci >= ri), same & (ci > ri)
        decay = jnp.where(incl, jnp.exp(jnp.where(incl, gcol - grow, 0.0)), 0.0)

        kbf = k.astype(BF16)
        kb = k * bcol
        eg = jnp.exp(gcol)
        rhs = jnp.concatenate([v * bcol, kb * eg], axis=1)
        kdec = (k * jnp.exp(tcol - gcol)).astype(BF16)
        egl = jnp.broadcast_to(jnp.exp(tcol), (GDN_BLOCK, LANES))
        kq = lax.dot_general(jnp.concatenate([kb, q], axis=0).astype(BF16), kbf, NT_DIMS,
                             preferred_element_type=F32)
        yield
        a = jnp.where(strict, kq[0:GDN_BLOCK, :] * decay, 0.0)
        qkm = (kq[GDN_BLOCK:, :] * decay).astype(BF16)
        assert GDN_CHUNK == 64
        a1 = a.astype(BF16)
        ima = eye - a
        a2f = _dot(a1, a1)
        yield
        a2 = a2f.astype(BF16)
        a4f = _dot(a2, a2)
        p1 = ima + _dot(ima.astype(BF16), a2)
        yield
        a4 = a4f.astype(BF16)
        a8f = _dot(a4, a4)
        yield
        a8 = a8f.astype(BF16)
        p2 = eye + a4f + a8f + _dot(a4, a8)
        a16 = _dot(a8, a8).astype(BF16)
        yield
        a32 = _dot(a16, a16).astype(BF16)
        p12 = _dot(p1.astype(BF16), p2.astype(BF16))
        x1 = rhs + _dot(a16, rhs.astype(BF16))
        yield
        x2 = x1 + _dot(a32, x1.astype(BF16))
        yield
        solb = _dot(p12.astype(BF16), x2.astype(BF16)).astype(BF16)
        yield
        intra = _dot(qkm, solb)
        bks = [lax.dot_general(kdec[c * GDN_CHUNK:(c + 1) * GDN_CHUNK, :], solb[c * GDN_CHUNK:(c + 1) * GDN_CHUNK, :],
                               TN_DIMS, preferred_element_type=F32) for c in range(nch)]
        yield
        eg_ref[slot] = egl
        oi_ref[slot] = intra[:, 0:GDN_DIM]
        qt = (q * eg - intra[:, GDN_DIM:]).astype(BF16)
        for c in range(nch):
            xs_ref[slot, c, 0:GDN_DIM, :] = bks[c][:, GDN_DIM:].astype(BF16)
            xs_ref[slot, c, GDN_DIM:GDN_DIM + GDN_CHUNK, :] = qt[c * GDN_CHUNK:(c + 1) * GDN_CHUNK, :]
            bs_ref[slot, c] = bks[c][:, 0:GDN_DIM]

    def recur_chunk(blk, direction, slot, c, out_ref):
        state = st_ref[direction]
        y = _dot(xs_ref[slot, c], state.astype(BF16))
        r0 = pl.multiple_of(blk * GDN_BLOCK, GDN_BLOCK)
        out_ref[pl.ds(r0 + c * GDN_CHUNK, GDN_CHUNK), :] = y[GDN_DIM:, :] + oi_ref[slot, c * GDN_CHUNK:(c + 1) * GDN_CHUNK, :]
        st_ref[direction] = state * eg_ref[slot, c * GDN_CHUNK:c * GDN_CHUNK + 1, :] + bs_ref[slot, c] - y[0:GDN_DIM, :]

    def step_blocks(j):
        return ((2 * j, 0, 0), (2 * j + 1, 0, 1), (nblk - 1 - 2 * j, 1, 2), (nblk - 2 - 2 * j, 1, 3))

    def recur_steps(j):
        (f0, _, _), (f1, _, _), (b0, _, _), (b1, _, _) = step_blocks(j)
        for fb, bb, fs, bs in ((f0, b0, 0, 2), (f1, b1, 1, 3)):
            for c in range(nch):
                yield lambda fb=fb, fs=fs, c=c: recur_chunk(fb, 0, fs, c, of_ref)
                yield lambda bb=bb, bs=bs, c=c: recur_chunk(bb, 1, bs, nch - 1 - c, xp_ref)

    def emit(prep_step, recur_step):
        gens = [prep(*b) for b in step_blocks(prep_step)] if prep_step is not None else []
        rec = list(recur_steps(recur_step)) if recur_step is not None else []
        n_levels = 9
        per_level = -(-len(rec) // (n_levels - 1)) if gens else len(rec)
        live = True
        while live or rec:
            live = False
            for g in gens:
                try:
                    next(g)
                    live = True
                except StopIteration:
                    pass
            last_level = bool(gens) and not live
            if not last_level:
                for f in rec[:per_level]:
                    f()
                rec = rec[per_level:]
            assert not (last_level and rec), "recurrence reads must all precede the scratch overwrite"

    nstep = nblk // 2
    emit(0, None)

    def main(j, c):
        emit(j, j - 1)
        return c

    lax.fori_loop(1, nstep, main, 0)
    emit(None, nstep - 1)

    def fin(i, c):
        r0 = pl.multiple_of(i * GDN_BLOCK, GDN_BLOCK)
        o = of_ref[pl.ds(r0, GDN_BLOCK), :] + xp_ref[pl.ds(r0, GDN_BLOCK), :]
        z = z_ref[0, pl.ds(r0, GDN_BLOCK), :].astype(F32)
        y = _rmsnorm_rows(o, onorm_ref[...]) * (z * _sigmoid(z))
        o_ref[0, pl.ds(r0, GDN_BLOCK), :] = y.astype(BF16)
        return c

    lax.fori_loop(0, nblk, fin, 0)


def _stage_c(proj3, gcb3, gct4, prm):
    b, s, _ = proj3.shape
    nblk = s // GDN_BLOCK
    nch = GDN_BLOCK // GDN_CHUNK
    assert nblk % 2 == 0

    def once(shape, imap):
        return pl.BlockSpec(shape, imap, pipeline_mode=pl.Buffered(1))

    kern = functools.partial(_gdn_kernel, seq=s)
    return pl.pallas_call(
        kern,
        grid=(b, GDN_HEADS),
        in_specs=[
            once((1, s, LANES), lambda bi, h: (bi, 0, COL_GQ + h)),
            once((1, s, LANES), lambda bi, h: (bi, 0, COL_GK + h)),
            once((1, s, LANES), lambda bi, h: (bi, 0, COL_GV + h)),
            once((1, s, LANES), lambda bi, h: (bi, 0, COL_GZ + h)),
            pl.BlockSpec((8, LANES), lambda bi, h: (0, h)),
            pl.BlockSpec((8, LANES), lambda bi, h: (0, GDN_HEADS + h)),
            pl.BlockSpec((8, LANES), lambda bi, h: (0, 2 * GDN_HEADS + h)),
            once((1, s, LANES), lambda bi, h: (bi, 0, 0)),
            once((1, nblk, 2 * GDN_HEADS, GDN_BLOCK), lambda bi, h: (bi, 0, 0, 0)),
            _const_spec((1, LANES)),
        ],
        out_specs=pl.BlockSpec((1, s, LANES), lambda bi, h: (bi, 0, h)),
        out_shape=jax.ShapeDtypeStruct((b, s, GDN_HEADS * GDN_DIM), BF16),
        scratch_shapes=[
            pltpu.VMEM((s + 16, LANES), F32),
            pltpu.VMEM((s, LANES), F32),
            pltpu.VMEM((s, LANES), F32),
            pltpu.VMEM((s, LANES), F32),
            pltpu.VMEM((s, LANES), F32),
            pltpu.VMEM((2, GDN_DIM, GDN_DIM), F32),
            pltpu.VMEM((4, nch, GDN_DIM + GDN_CHUNK, GDN_DIM), BF16),
            pltpu.VMEM((4, nch, GDN_DIM, GDN_DIM), F32),
            pltpu.VMEM((4, GDN_BLOCK, GDN_DIM), F32),
            pltpu.VMEM((4, GDN_BLOCK, LANES), F32),
        ],
        compiler_params=_cparams("parallel", "arbitrary"),
        name="stage_c_gdn",
    )(proj3, proj3, proj3, proj3, prm["conv_w"], prm["conv_w"], prm["conv_w"], gcb3, gct4, prm["out_norm"])


def _merge_kernel(ya_ref, yb_ref, ga0_ref, ga1_ref, gb0_ref, gb1_ref, x_ref, wa_ref, wb_ref, wo_ref, o_ref):
    half = D_MODEL // 2
    pa = _dot(ya_ref[...], wa_ref[...])
    pb = _dot(yb_ref[...], wb_ref[...])
    m0 = ga0_ref[...].astype(F32) * pa[:, 0:half] + gb0_ref[...].astype(F32) * pb[:, 0:half]
    m1 = ga1_ref[...].astype(F32) * pa[:, half:] + gb1_ref[...].astype(F32) * pb[:, half:]
    mixed = jnp.concatenate([m0, m1], axis=1).astype(BF16)
    o_ref[...] = x_ref[...] + _dot(mixed, wo_ref[...])


def _stage_d(ya2d, yb2d, proj2d, x2d, prm, tm):
    t = x2d.shape[0]
    half = D_MODEL // 2
    gate0 = (COL_GZ + GDN_HEADS) * LANES // half
    row = lambda c: pl.BlockSpec((tm, half), lambda i: (i, c))
    return pl.pallas_call(
        _merge_kernel,
        grid=(t // tm,),
        in_specs=[
            row(0), row(0), row(gate0), row(gate0 + 1), row(gate0 + 2), row(gate0 + 3),
            pl.BlockSpec((tm, D_MODEL), lambda i: (i, 0)),
            _const_spec((half, D_MODEL)),
            _const_spec((half, D_MODEL)),
            _const_spec((D_MODEL, D_MODEL)),
        ],
        out_specs=pl.BlockSpec((tm, D_MODEL), lambda i: (i, 0)),
        out_shape=jax.ShapeDtypeStruct((t, D_MODEL), F32),
        compiler_params=_cparams("parallel"),
        name="stage_d_merge",
    )(ya2d, yb2d, proj2d, proj2d, proj2d, proj2d, x2d, prm["w_branch_a"], prm["w_branch_b"], prm["w_out"])


def _ffn_kernel(x_ref, g_ref, wu_ref, wg_ref, wo_ref, o_ref, hn_ref):
    f = pl.program_id(1)

    @pl.when(f == 0)
    def _():
        x = x_ref[...]
        hn_ref[...] = _rmsnorm_rows(x, g_ref[...]).astype(BF16)
        o_ref[...] = x

    hn = hn_ref[...]
    up = _dot(hn, wu_ref[...])
    gt = _dot(hn, wg_ref[...])
    act = (gt * _sigmoid(gt) * up).astype(BF16)
    o_ref[...] += _dot(act, wo_ref[...])


def _stage_e(x2d, prm, tm, tf):
    t = x2d.shape[0]
    nf = FFN_HIDDEN // tf
    return pl.pallas_call(
        _ffn_kernel,
        grid=(t // tm, nf),
        in_specs=[
            pl.BlockSpec((tm, D_MODEL), lambda i, f: (i, 0)),
            _const_spec((1, D_MODEL)),
            pl.BlockSpec((D_MODEL, tf), lambda i, f: (0, f)),
            pl.BlockSpec((D_MODEL, tf), lambda i, f: (0, nf + f)),
            pl.BlockSpec((tf, D_MODEL), lambda i, f: (f, 0)),
        ],
        out_specs=pl.BlockSpec((tm, D_MODEL), lambda i, f: (i, 0)),
        out_shape=jax.ShapeDtypeStruct((t, D_MODEL), F32),
        scratch_shapes=[pltpu.VMEM((tm, D_MODEL), BF16)],
        compiler_params=_cparams("parallel", "arbitrary"),
        name="stage_e_ffn",
    )(x2d, prm["norm_ffn"], prm["w_ffn_in"], prm["w_ffn_in"], prm["w_ffn_out"])


def _ple_kernel(x_ref, g_ref, wg_ref, p_ref, wp_ref, o_ref, hn_ref, *, tn):
    j = pl.program_id(1)

    @pl.when(j == 0)
    def _():
        hn_ref[...] = _rmsnorm_rows(x_ref[...], g_ref[...]).astype(BF16)

    gate = _sigmoid(_dot(hn_ref[...], wg_ref[...]))
    emb = _dot(p_ref[...].astype(BF16), wp_ref[...])
    c0 = pl.multiple_of(j * tn, tn)
    o_ref[...] = x_ref[:, pl.ds(c0, tn)] + gate * emb


def _stage_f(x2d, p2d, prm, tm, tn):
    t = x2d.shape[0]
    kern = functools.partial(_ple_kernel, tn=tn)
    return pl.pallas_call(
        kern,
        grid=(t // tm, D_MODEL // tn),
        in_specs=[
            pl.BlockSpec((tm, D_MODEL), lambda i, j: (i, 0)),
            _const_spec((1, D_MODEL)),
            pl.BlockSpec((D_MODEL, tn), lambda i, j: (0, j)),
            pl.BlockSpec((tm, PLE_DIM), lambda i, j: (i, 0)),
            pl.BlockSpec((PLE_DIM, tn), lambda i, j: (0, j)),
        ],
        out_specs=pl.BlockSpec((tm, tn), lambda i, j: (i, j)),
        out_shape=jax.ShapeDtypeStruct((t, D_MODEL), F32),
        scratch_shapes=[pltpu.VMEM((tm, D_MODEL), BF16)],
        compiler_params=_cparams("parallel", "arbitrary"),
        name="stage_f_ple",
    )(x2d, prm["norm_ple"], prm["w_ple_gate"], p2d, prm["w_ple_proj"])


def _block_diag_mask(n, blk, kind):
    r = jnp.arange(n)[:, None]
    c = jnp.arange(n)[None, :]
    same = (r // blk) == (c // blk)
    if kind == "lower":
        same = same & (c <= r)
    elif kind == "upper":
        same = same & (c >= r)
    return same.astype(BF16)


def _prepare_params(norm_mix, w_in, da_q_norm, da_k_norm, lambda_q1, lambda_k1, lambda_q2, lambda_k2,
                    da_subln, gdn_conv, gdn_a_log, gdn_dt_bias, gdn_out_norm, w_branch_a, w_branch_b,
                    w_out, norm_ffn, w_ffn_in, w_ffn_out, norm_ple, w_ple_gate, w_ple_proj):
    n_fixed = COL_GZ * LANES + GDN_HEADS * GDN_DIM
    n_ab = 4 * GDN_HEADS
    row = lambda v: v.reshape(1, -1).astype(F32)
    return {
        "norm_mix": row(norm_mix),
        "w_main": jnp.concatenate([w_in[:, :n_fixed], w_in[:, n_fixed + n_ab:]], axis=1).astype(BF16),
        "w_ab": jnp.pad(w_in[:, n_fixed:n_fixed + n_ab], ((0, 0), (0, LANES - n_ab))).astype(BF16),
        "grp64": _block_diag_mask(MXU_DIM, DA_HEAD_DIM, "full"),
        "q_gain": row(jnp.tile(da_q_norm, MXU_DIM // DA_HEAD_DIM)),
        "k_gain": row(jnp.tile(da_k_norm, MXU_DIM // DA_HEAD_DIM)),
        "slopes": (2.0 ** (-8.0 * jnp.arange(1, DA_HEADS + 1, dtype=F32) / DA_HEADS)) * LOG2E,
        "lam_params": jnp.stack([lambda_q1, lambda_k1, lambda_q2, lambda_k2]).astype(F32),
        "subln": row(da_subln),
        "conv_w": jnp.pad(gdn_conv.astype(F32), ((0, 8 - CONV_K), (0, 0))),
        "alog_row": jnp.pad(gdn_a_log.reshape(1, -1).astype(F32), ((0, 0), (0, LANES - 2 * GDN_HEADS))),
        "dtb_row": jnp.pad(gdn_dt_bias.reshape(1, -1).astype(F32), ((0, 0), (0, LANES - 2 * GDN_HEADS))),
        "ltri": _block_diag_mask(GDN_BLOCK, GDN_CHUNK, "lower"),
        "utri": _block_diag_mask(GDN_BLOCK, GDN_CHUNK, "upper"),
        "ones_bd": _block_diag_mask(GDN_BLOCK, GDN_CHUNK, "full"),
        "out_norm": row(gdn_out_norm),
        "w_branch_a": w_branch_a.astype(BF16),
        "w_branch_b": w_branch_b.astype(BF16),
        "w_out": w_out.astype(BF16),
        "norm_ffn": row(norm_ffn),
        "w_ffn_in": w_ffn_in.astype(BF16),
        "w_ffn_out": w_ffn_out.astype(BF16),
        "norm_ple": row(norm_ple),
        "w_ple_gate": w_ple_gate.astype(BF16),
        "w_ple_proj": w_ple_proj.astype(BF16),
    }


def _tile(n, pref):
    return pref if n % pref == 0 else n


def _encoder_layer(x, p, prm):
    b, s, _ = x.shape
    t = b * s
    x2d = x.reshape(t, D_MODEL)
    proj, ab = _stage_a(x2d, prm, _tile(t, 1024))
    proj3 = proj.reshape(b, s, PROJ_MAIN_W)
    ya = _stage_b(proj3, prm, _tile(s, 512), _tile(s, 512))
    gcb, gct = _stage_c0(ab, prm)
    yb = _stage_c(proj3, gcb.reshape(b, s, LANES), gct.reshape(b, s // GDN_BLOCK, LANES, GDN_BLOCK), prm)
    x1 = _stage_d(ya.reshape(t, -1), yb.reshape(t, -1), proj, x2d, prm, _tile(t, 256))
    x2 = _stage_e(x1, prm, _tile(t, 512), 512)
    x3 = _stage_f(x2, p.reshape(t, PLE_DIM), prm, _tile(t, 512), 1024)
    return x3.reshape(b, s, D_MODEL)


def kernel(x_prompt, x_sample, p_prompt, p_sample, norm_mix, w_in, da_q_norm, da_k_norm, lambda_q1, lambda_k1, lambda_q2, lambda_k2, da_subln, gdn_conv, gdn_a_log, gdn_dt_bias, gdn_out_norm, w_branch_a, w_branch_b, w_out, norm_ffn, w_ffn_in, w_ffn_out, norm_ple, w_ple_gate, w_ple_proj):
    layer_params = (norm_mix, w_in, da_q_norm, da_k_norm, lambda_q1, lambda_k1, lambda_q2, lambda_k2,
                    da_subln, gdn_conv, gdn_a_log, gdn_dt_bias, gdn_out_norm, w_branch_a, w_branch_b,
                    w_out, norm_ffn, w_ffn_in, w_ffn_out, norm_ple, w_ple_gate, w_ple_proj)
    depth = norm_mix.shape[0]
    assert depth == 1, "LAM_INIT is the depth-0 value"
    y_prompt, y_sample = x_prompt, x_sample
    for layer in range(depth):
        prm = _prepare_params(*(w[layer] for w in layer_params))
        y_prompt = _encoder_layer(y_prompt, p_prompt[layer], prm)
        y_sample = _encoder_layer(y_sample, p_sample[layer], prm)
    return (y_prompt, y_sample)
```

```python
import functools
import math

import jax
import jax.numpy as jnp
from jax import lax
from jax.experimental import pallas as pl
from jax.experimental.pallas import tpu as pltpu

F32 = jnp.float32
BF16 = jnp.bfloat16

D_MODEL = 2048
DA_HEADS = 8
DA_HEAD_DIM = 64
GDN_HEADS = 8
GDN_DIM = 128
GDN_CHUNK = 64
CONV_K = 5
FFN_HIDDEN = 5632
PLE_DIM = 256
NORM_EPS = 1e-6
LAM_INIT = 0.8 - 0.6 * math.exp(-0.3 * 0)
LOG2E = math.log2(math.e)

LANES = 128
MXU_DIM = 256
GDN_BLOCK = 256
VMEM_LIMIT = 56 * 1024 * 1024
SKIP_LOG2 = 160.0

COL_Q, COL_K, COL_V = 0, 8, 16
COL_GQ, COL_GK, COL_GV, COL_GZ = 24, 32, 40, 48
PROJ_MAIN_W = 11264
PROJ_TN = 1024

NT_DIMS = (((1,), (1,)), ((), ()))
TN_DIMS = (((0,), (0,)), ((), ()))


def _cparams(*sem):
    return pltpu.CompilerParams(dimension_semantics=sem, vmem_limit_bytes=VMEM_LIMIT)


def _const_spec(shape):
    nd = len(shape)
    return pl.BlockSpec(shape, lambda *_: (0,) * nd, pipeline_mode=pl.Buffered(1))


def _rmsnorm_rows(x, gain):
    ms = jnp.mean(x * x, axis=-1, keepdims=True)
    return x * lax.rsqrt(ms + NORM_EPS) * gain


def _sigmoid(x):
    return 1.0 / (1.0 + jnp.exp(-x))


def _split3(x):
    hi = x.astype(BF16)
    r1 = x - hi.astype(F32)
    mid = r1.astype(BF16)
    lo = (r1 - mid.astype(F32)).astype(BF16)
    return hi, mid, lo


def _dot(a, b):
    return jnp.dot(a, b, preferred_element_type=F32)


def _proj_kernel(x_ref, g_ref, w_ref, wab_ref, grp_ref, qg_ref, kg_ref, o_ref, ab_ref, hn_ref):
    j = pl.program_id(1)

    @pl.when(j == 0)
    def _():
        hb = _rmsnorm_rows(x_ref[...], g_ref[...]).astype(BF16)
        hn_ref[...] = hb
        ab_ref[...] = _dot(hb, wab_ref[...])

    @pl.when(j < 2)
    def _():
        acc = _dot(hn_ref[...], w_ref[...])
        gain = jnp.where(j == 0, qg_ref[...] * (DA_HEAD_DIM ** -0.5 * LOG2E), kg_ref[...])
        for c in range(PROJ_TN // MXU_DIM):
            a = acc[:, c * MXU_DIM:(c + 1) * MXU_DIM]
            sq = a * a
            hi = sq.astype(BF16)
            lo = (sq - hi.astype(F32)).astype(BF16)
            ss = _dot(hi, grp_ref[...]) + _dot(lo, grp_ref[...])
            y = a * lax.rsqrt(ss * (1.0 / DA_HEAD_DIM) + NORM_EPS) * gain
            o_ref[:, c * MXU_DIM:(c + 1) * MXU_DIM] = y.astype(BF16)

    @pl.when(j >= 2)
    def _():
        acc = _dot(hn_ref[...], w_ref[...])
        o_ref[...] = jnp.where(j >= 7, _sigmoid(acc), acc).astype(BF16)


def _stage_a(x2d, prm, tm):
    t = x2d.shape[0]
    return pl.pallas_call(
        _proj_kernel,
        grid=(t // tm, PROJ_MAIN_W // PROJ_TN),
        in_specs=[
            pl.BlockSpec((tm, D_MODEL), lambda i, j: (i, 0)),
            _const_spec((1, D_MODEL)),
            pl.BlockSpec((D_MODEL, PROJ_TN), lambda i, j: (0, j)),
            _const_spec((D_MODEL, LANES)),
            _const_spec((MXU_DIM, MXU_DIM)),
            _const_spec((1, MXU_DIM)),
            _const_spec((1, MXU_DIM)),
        ],
        out_specs=[
            pl.BlockSpec((tm, PROJ_TN), lambda i, j: (i, j)),
            pl.BlockSpec((tm, LANES), lambda i, j: (i, 0)),
        ],
        out_shape=[
            jax.ShapeDtypeStruct((t, PROJ_MAIN_W), BF16),
            jax.ShapeDtypeStruct((t, LANES), F32),
        ],
        scratch_shapes=[pltpu.VMEM((tm, D_MODEL), BF16)],
        compiler_params=_cparams("parallel", "arbitrary"),
        name="stage_a_proj",
    )(x2d, prm["norm_mix"], prm["w_main"], prm["w_ab"], prm["grp64"], prm["q_gain"], prm["k_gain"])


def _attn_kernel(slopes_ref, q_ref, k_ref, v_ref, lam_ref, subln_ref, o_ref,
                 kx_ref, vx_ref, qx_ref, sa_ref, sb_ref, m_ref, acc_ref, kn2_ref, *, tq, tk, seq):
    h = pl.program_id(1)
    qi = pl.program_id(2)
    slope = slopes_ref[h]

    def split3_f32(x):
        return tuple(p.astype(F32) for p in _split3(x))

    def ext_lanes(shape, pieces, unit):
        lane = lax.broadcasted_iota(jnp.int32, shape, 1)
        hi, mid, lo = pieces
        return jnp.where(lane == 0, hi, jnp.where(lane == 1, mid, jnp.where(lane == 2, lo,
                         jnp.where(lane < 6, unit, 0.0)))).astype(BF16)

    def max_map_norm2(x):
        lane = lax.broadcasted_iota(jnp.int32, x.shape, 1)
        sq = x.astype(F32) * x.astype(F32)
        n0 = jnp.sum(jnp.where(lane < DA_HEAD_DIM, sq, 0.0), axis=-1, keepdims=True)
        n1 = jnp.sum(jnp.where(lane >= DA_HEAD_DIM, sq, 0.0), axis=-1, keepdims=True)
        return jnp.max(jnp.maximum(n0, n1), axis=0, keepdims=True)

    @pl.when(qi == 0)
    def _():
        def fill(i, c):
            r0 = pl.multiple_of(i * tk, tk)
            pos = (r0 + lax.broadcasted_iota(jnp.int32, (tk, LANES), 0)).astype(F32)
            hi, mid, lo = split3_f32(slope * pos)
            lane = lax.broadcasted_iota(jnp.int32, (tk, LANES), 1)
            kext = jnp.where(lane < 3, 1.0, jnp.where(lane == 3, hi, jnp.where(lane == 4, mid,
                             jnp.where(lane == 5, lo, 0.0)))).astype(BF16)
            kb = k_ref[0, pl.ds(r0, tk), :]
            kx_ref[pl.ds(r0, tk), 0:LANES] = kb
            kx_ref[pl.ds(r0, tk), LANES:2 * LANES] = kext
            kn2_ref[...] = jnp.maximum(kn2_ref[...], max_map_norm2(kb))
            vx_ref[pl.ds(r0, tk), 0:LANES] = v_ref[0, pl.ds(r0, tk), :]
            vx_ref[pl.ds(r0, tk), LANES:2 * LANES] = jnp.ones((tk, LANES), BF16)
            return c

        kn2_ref[...] = jnp.zeros(kn2_ref.shape, F32)
        lax.fori_loop(0, seq // tk, fill, 0)

    q0 = qi * tq
    q = q_ref[0]
    lane = lax.broadcasted_iota(jnp.int32, (tq, LANES), 1)
    qm0 = jnp.where(lane < DA_HEAD_DIM, q, jnp.zeros_like(q))
    qm1 = jnp.where(lane >= DA_HEAD_DIM, q, jnp.zeros_like(q))
    pos = (q0 + lax.broadcasted_iota(jnp.int32, (tq, LANES), 0)).astype(F32)
    hi, mid, lo = split3_f32(slope * pos)
    ext_l = ext_lanes((tq, LANES), (-hi, -mid, -lo), 1.0)
    ext_r = ext_lanes((tq, LANES), (hi, mid, lo), -1.0)
    for side, ext in ((0, ext_l), (1, ext_r)):
        qx_ref[side, 0:tq, 0:LANES] = qm0
        qx_ref[side, tq:2 * tq, 0:LANES] = qm1
        qx_ref[side, 0:tq, LANES:2 * LANES] = ext
        qx_ref[side, tq:2 * tq, LANES:2 * LANES] = ext
    m_ref[...] = jnp.full(m_ref.shape, -1e30, F32)
    acc_ref[...] = jnp.zeros(acc_ref.shape, F32)

    nk = seq // tk
    n_left = q0 // tk

    bound = jnp.sqrt(max_map_norm2(q) * kn2_ref[0:1, 0:1])
    w_f = jnp.minimum((SKIP_LOG2 + 2.0 * bound) / slope, float(seq))
    w = jnp.ceil(w_f).astype(jnp.int32)[0, 0]
    lo = jnp.minimum(jnp.maximum(q0 - w, 0) // tk, n_left)
    hi = jnp.minimum((w + q0 + tq - 2) // tk + 1, nk)
    n_rest = (n_left - lo) + (hi - n_left - 1)

    def kblock(ki):
        return kx_ref[pl.ds(pl.multiple_of(ki * tk, tk), tk), :]

    def scores(ki):
        ki = jnp.clip(ki, 0, nk - 1)
        side = jnp.where(ki >= n_left, 1, 0)
        return lax.dot_general(qx_ref[side], kblock(ki), NT_DIMS, preferred_element_type=F32)

    def consume(s, ki):
        m_old = m_ref[...]
        m_new = jnp.maximum(m_old, jnp.max(s, axis=-1, keepdims=True))
        alpha = jnp.exp2(m_old - m_new)
        p = jnp.exp2(s - jnp.concatenate([m_new] * (tk // LANES), axis=1))
        vx = vx_ref[pl.ds(pl.multiple_of(ki * tk, tk), tk), :]
        pv = _dot(p.astype(BF16), vx)
        acc_ref[...] = jnp.concatenate([alpha, alpha], axis=1) * acc_ref[...] + pv
        m_ref[...] = m_new

    kd = kblock(n_left)
    s_diag = jnp.minimum(lax.dot_general(qx_ref[0], kd, NT_DIMS, preferred_element_type=F32),
                         lax.dot_general(qx_ref[1], kd, NT_DIMS, preferred_element_type=F32))
    nth = lambda j: jnp.where(lo + j >= n_left, lo + j + 1, lo + j)
    sa_ref[...] = scores(nth(0))
    consume(s_diag, n_left)

    def body(t, carry):
        j = 2 * t
        sb_ref[...] = scores(nth(j + 1))
        consume(sa_ref[...], nth(j))
        sa_ref[...] = scores(nth(j + 2))
        consume(sb_ref[...], nth(j + 1))
        return carry

    lax.fori_loop(0, n_rest // 2, body, 0)

    @pl.when(n_rest % 2 == 1)
    def _():
        consume(sa_ref[...], nth(n_rest - 1))

    acc = acc_ref[...]
    o = acc[:, 0:LANES] * (1.0 / acc[:, LANES:2 * LANES])
    lp = lam_ref[...]
    lam = (jnp.exp(jnp.sum(lp[0:1, :] * lp[1:2, :], axis=-1, keepdims=True))
           - jnp.exp(jnp.sum(lp[2:3, :] * lp[3:4, :], axis=-1, keepdims=True)) + LAM_INIT)
    d = o[0:tq, :] - lam * o[tq:2 * tq, :]
    y = _rmsnorm_rows(d, subln_ref[...]) * (1.0 - LAM_INIT)
    o_ref[0] = y.astype(BF16)


def _stage_b(proj3, prm, tq, tk):
    b, s, _ = proj3.shape
    assert tk % tq == 0 and s % tk == 0, "each query block must sit inside one key block"
    kern = functools.partial(_attn_kernel, tq=tq, tk=tk, seq=s)
    return pl.pallas_call(
        kern,
        grid=(b, DA_HEADS, s // tq),
        in_specs=[
            pl.BlockSpec(memory_space=pltpu.SMEM),
            pl.BlockSpec((1, tq, LANES), lambda bi, h, qi: (bi, qi, COL_Q + h)),
            pl.BlockSpec((1, s, LANES), lambda bi, h, qi: (bi, 0, COL_K + h)),
            pl.BlockSpec((1, s, LANES), lambda bi, h, qi: (bi, 0, COL_V + h)),
            _const_spec((4, DA_HEAD_DIM)),
            _const_spec((1, LANES)),
        ],
        out_specs=pl.BlockSpec((1, tq, LANES), lambda bi, h, qi: (bi, qi, h)),
        out_shape=jax.ShapeDtypeStruct((b, s, DA_HEADS * LANES), BF16),
        scratch_shapes=[
            pltpu.VMEM((s, 2 * LANES), BF16),
            pltpu.VMEM((s, 2 * LANES), BF16),
            pltpu.VMEM((2, 2 * tq, 2 * LANES), BF16),
            pltpu.VMEM((2 * tq, tk), F32),
            pltpu.VMEM((2 * tq, tk), F32),
            pltpu.VMEM((2 * tq, LANES), F32),
            pltpu.VMEM((2 * tq, 2 * LANES), F32),
            pltpu.VMEM((8, LANES), F32),
        ],
        compiler_params=_cparams("parallel", "parallel", "arbitrary"),
        name="stage_b_diff_attn",
    )(prm["slopes"], proj3, proj3, proj3, prm["lam_params"], prm["subln"])


def _softplus(x):
    return jnp.maximum(x, 0.0) + jnp.log1p(jnp.exp(-jnp.abs(x)))


def _gdn_gate_kernel(ab_ref, alog_ref, dtb_ref, ltri_ref, utri_ref, ones_ref, o_ref, ot_ref):
    x = ab_ref[...]
    g = -jnp.exp(alog_ref[...]) * _softplus(x + dtb_ref[...])
    parts = _split3(g)
    cf = sum(_dot(ltri_ref[...], p) for p in parts)
    cb = sum(_dot(utri_ref[...], p) for p in parts)
    tot = sum(_dot(ones_ref[...], p) for p in parts)
    lane = lax.broadcasted_iota(jnp.int32, x.shape, 1)
    out = jnp.where(lane < 8, cf,
                    jnp.where(lane < 16, cb,
                              jnp.where(lane < 32, _sigmoid(x), pltpu.roll(tot, 32, axis=1))))
    o_ref[...] = out
    ot_ref[0] = out.T


def _stage_c0(ab2d, prm):
    t = ab2d.shape[0]
    nb = t // GDN_BLOCK
    return pl.pallas_call(
        _gdn_gate_kernel,
        grid=(nb,),
        in_specs=[
            pl.BlockSpec((GDN_BLOCK, LANES), lambda i: (i, 0)),
            _const_spec((1, LANES)),
            _const_spec((1, LANES)),
            _const_spec((GDN_BLOCK, GDN_BLOCK)),
            _const_spec((GDN_BLOCK, GDN_BLOCK)),
            _const_spec((GDN_BLOCK, GDN_BLOCK)),
        ],
        out_specs=[
            pl.BlockSpec((GDN_BLOCK, LANES), lambda i: (i, 0)),
            pl.BlockSpec((1, LANES, GDN_BLOCK), lambda i: (i, 0, 0)),
        ],
        out_shape=[
            jax.ShapeDtypeStruct((t, LANES), F32),
            jax.ShapeDtypeStruct((nb, LANES, GDN_BLOCK), F32),
        ],
        compiler_params=_cparams("parallel"),
        name="stage_c0_gdn_gates",
    )(ab2d, prm["alog_row"], prm["dtb_row"], prm["ltri"], prm["utri"], prm["ones_bd"])


def _gdn_kernel(qp_ref, kp_ref, vp_ref, z_ref, wq_ref, wk_ref, wv_ref, gcb_ref, gct_ref, onorm_ref,
                o_ref, xp_ref, qn_ref, kn_ref, vn_ref, of_ref, st_ref, xs_ref, bs_ref, oi_ref, eg_ref, *, seq):
    h = pl.program_id(1)
    nblk = seq // GDN_BLOCK
    nch = GDN_BLOCK // GDN_CHUNK
    pad = 8
    rows = 2 * GDN_BLOCK

    def conv_phase(src_ref, w_ref, dst_ref, mode):
        xp_ref[0:pad, :] = jnp.zeros((pad, LANES), F32)
        xp_ref[pad + seq:pad + seq + pad, :] = jnp.zeros((pad, LANES), F32)

        def fill(i, c):
            r0 = pl.multiple_of(i * rows, rows)
            xp_ref[pl.ds(pad + r0, rows), :] = src_ref[0, pl.ds(r0, rows), :].astype(F32)
            return c

        lax.fori_loop(0, seq // rows, fill, 0)

        def conv(i, c):
            r0 = pl.multiple_of(i * rows, rows)
            acc = jnp.zeros((rows, LANES), F32)
            for t in range(CONV_K):
                acc = acc + w_ref[t:t + 1, :] * xp_ref[pl.ds(r0 + pad + t - (CONV_K - 1) // 2, rows), :]
            y = acc * _sigmoid(acc)
            if mode != "v":
                y = y * lax.rsqrt(jnp.sum(y * y, axis=-1, keepdims=True) + NORM_EPS)
            if mode == "q":
                y = y * (GDN_DIM ** -0.5)
            dst_ref[pl.ds(r0, rows), :] = y
            return c

        lax.fori_loop(0, seq // rows, conv, 0)

    conv_phase(qp_ref, wq_ref, qn_ref, "q")
    conv_phase(kp_ref, wk_ref, kn_ref, "k")
    conv_phase(vp_ref, wv_ref, vn_ref, "v")

    st_ref[...] = jnp.zeros(st_ref.shape, F32)

    ri = lax.broadcasted_iota(jnp.int32, (GDN_BLOCK, GDN_BLOCK), 0)
    ci = lax.broadcasted_iota(jnp.int32, (GDN_BLOCK, GDN_BLOCK), 1)
    same = (ri // GDN_CHUNK) == (ci // GDN_CHUNK)
    eye = (ri == ci).astype(F32)
    lane = lax.broadcasted_iota(jnp.int32, (GDN_BLOCK, LANES), 1)

    def pick(gblk, col):
        return jnp.sum(jnp.where(lane == col, gblk, 0.0), axis=-1, keepdims=True)

    def prep(blk, direction, slot):
        r0 = pl.multiple_of(blk * GDN_BLOCK, GDN_BLOCK)
        q = qn_ref[pl.ds(r0, GDN_BLOCK), :]
        k = kn_ref[pl.ds(r0, GDN_BLOCK), :]
        v = vn_ref[pl.ds(r0, GDN_BLOCK), :]
        gblk = gcb_ref[0, pl.ds(r0, GDN_BLOCK), :]
        col = direction * GDN_HEADS + h
        gcol = pick(gblk, col)
        bcol = pick(gblk, 16 + col)
        tcol = pick(gblk, 32 + col)
        grow = gct_ref[0, blk, pl.ds(col, 1), :]
        if direction == 0:
            incl, strict = same & (ci <= ri), same & (ci < ri)
        else:
            incl, strict = same & (ci >= ri), same & (ci > ri)
        decay = jnp.where(incl, jnp.exp(jnp.where(incl, gcol - grow, 0.0)), 0.0)

        kbf = k.astype(BF16)
        kb = k * bcol
        eg = jnp.exp(gcol)
        rhs = jnp.concatenate([v * bcol, kb * eg], axis=1)
        kdec = (k * jnp.exp(tcol - gcol)).astype(BF16)
        egl = jnp.broadcast_to(jnp.exp(tcol), (GDN_BLOCK, LANES))
        kq = lax.dot_general(jnp.concatenate([kb, q], axis=0).astype(BF16), kbf, NT_DIMS,
                             preferred_element_type=F32)
        yield
        a = jnp.where(strict, kq[0:GDN_BLOCK, :] * decay, 0.0)
        qkm = (kq[GDN_BLOCK:, :] * decay).astype(BF16)
        assert GDN_CHUNK == 64
        a1 = a.astype(BF16)
        ima = eye - a
        a2f = _dot(a1, a1)
        yield
        a2 = a2f.astype(BF16)
        a4f = _dot(a2, a2)
        p1 = ima + _dot(ima.astype(BF16), a2)
        yield
        a4 = a4f.astype(BF16)
        a8f = _dot(a4, a4)
        yield
        a8 = a8f.astype(BF16)
        p2 = eye + a4f + a8f + _dot(a4, a8)
        a16 = _dot(a8, a8).astype(BF16)
        yield
        a32 = _dot(a16, a16).astype(BF16)
        p12 = _dot(p1.astype(BF16), p2.astype(BF16))
        x1 = rhs + _dot(a16, rhs.astype(BF16))
        yield
        x2 = x1 + _dot(a32, x1.astype(BF16))
        yield
        solb = _dot(p12.astype(BF16), x2.astype(BF16)).astype(BF16)
        yield
        intra = _dot(qkm, solb)
        bks = [lax.dot_general(kdec[c * GDN_CHUNK:(c + 1) * GDN_CHUNK, :], solb[c * GDN_CHUNK:(c + 1) * GDN_CHUNK, :],
                               TN_DIMS, preferred_element_type=F32) for c in range(nch)]
        yield
        eg_ref[slot] = egl
        oi_ref[slot] = intra[:, 0:GDN_DIM]
        qt = (q * eg - intra[:, GDN_DIM:]).astype(BF16)
        for c in range(nch):
            xs_ref[slot, c, 0:GDN_DIM, :] = bks[c][:, GDN_DIM:].astype(BF16)
            xs_ref[slot, c, GDN_DIM:GDN_DIM + GDN_CHUNK, :] = qt[c * GDN_CHUNK:(c + 1) * GDN_CHUNK, :]
            bs_ref[slot, c] = bks[c][:, 0:GDN_DIM]

    def recur_chunk(blk, direction, slot, c, out_ref):
        state = st_ref[direction]
        y = _dot(xs_ref[slot, c], state.astype(BF16))
        r0 = pl.multiple_of(blk * GDN_BLOCK, GDN_BLOCK)
        out_ref[pl.ds(r0 + c * GDN_CHUNK, GDN_CHUNK), :] = y[GDN_DIM:, :] + oi_ref[slot, c * GDN_CHUNK:(c + 1) * GDN_CHUNK, :]
        st_ref[direction] = state * eg_ref[slot, c * GDN_CHUNK:c * GDN_CHUNK + 1, :] + bs_ref[slot, c] - y[0:GDN_DIM, :]

    def step_blocks(j):
        return ((2 * j, 0, 0), (2 * j + 1, 0, 1), (nblk - 1 - 2 * j, 1, 2), (nblk - 2 - 2 * j, 1, 3))

    def recur_steps(j):
        (f0, _, _), (f1, _, _), (b0, _, _), (b1, _, _) = step_blocks(j)
        for fb, bb, fs, bs in ((f0, b0, 0, 2), (f1, b1, 1, 3)):
            for c in range(nch):
                yield lambda fb=fb, fs=fs, c=c: recur_chunk(fb, 0, fs, c, of_ref)
                yield lambda bb=bb, bs=bs, c=c: recur_chunk(bb, 1, bs, nch - 1 - c, xp_ref)

    def emit(prep_step, recur_step):
        gens = [prep(*b) for b in step_blocks(prep_step)] if prep_step is not None else []
        rec = list(recur_steps(recur_step)) if recur_step is not None else []
        n_levels = 9
        per_level = -(-len(rec) // (n_levels - 1)) if gens else len(rec)
        live = True
        while live or rec:
            live = False
            for g in gens:
                try:
                    next(g)
                    live = True
                except StopIteration:
                    pass
            last_level = bool(gens) and not live
            if not last_level:
                for f in rec[:per_level]:
                    f()
                rec = rec[per_level:]
            assert not (last_level and rec), "recurrence reads must all precede the scratch overwrite"

    nstep = nblk // 2
    emit(0, None)

    def main(j, c):
        emit(j, j - 1)
        return c

    lax.fori_loop(1, nstep, main, 0)
    emit(None, nstep - 1)

    def fin(i, c):
        r0 = pl.multiple_of(i * rows, rows)
        o = of_ref[pl.ds(r0, rows), :] + xp_ref[pl.ds(r0, rows), :]
        z = z_ref[0, pl.ds(r0, rows), :].astype(F32)
        y = _rmsnorm_rows(o, onorm_ref[...]) * (z * _sigmoid(z))
        o_ref[0, pl.ds(r0, rows), :] = y.astype(BF16)
        return c

    lax.fori_loop(0, seq // rows, fin, 0)


def _stage_c(proj3, gcb3, gct4, prm):
    b, s, _ = proj3.shape
    nblk = s // GDN_BLOCK
    nch = GDN_BLOCK // GDN_CHUNK
    assert nblk % 2 == 0

    def once(shape, imap):
        return pl.BlockSpec(shape, imap, pipeline_mode=pl.Buffered(1))

    kern = functools.partial(_gdn_kernel, seq=s)
    return pl.pallas_call(
        kern,
        grid=(b, GDN_HEADS),
        in_specs=[
            once((1, s, LANES), lambda bi, h: (bi, 0, COL_GQ + h)),
            once((1, s, LANES), lambda bi, h: (bi, 0, COL_GK + h)),
            once((1, s, LANES), lambda bi, h: (bi, 0, COL_GV + h)),
            once((1, s, LANES), lambda bi, h: (bi, 0, COL_GZ + h)),
            pl.BlockSpec((8, LANES), lambda bi, h: (0, h)),
            pl.BlockSpec((8, LANES), lambda bi, h: (0, GDN_HEADS + h)),
            pl.BlockSpec((8, LANES), lambda bi, h: (0, 2 * GDN_HEADS + h)),
            once((1, s, LANES), lambda bi, h: (bi, 0, 0)),
            once((1, nblk, 2 * GDN_HEADS, GDN_BLOCK), lambda bi, h: (bi, 0, 0, 0)),
            _const_spec((1, LANES)),
        ],
        out_specs=pl.BlockSpec((1, s, LANES), lambda bi, h: (bi, 0, h)),
        out_shape=jax.ShapeDtypeStruct((b, s, GDN_HEADS * GDN_DIM), BF16),
        scratch_shapes=[
            pltpu.VMEM((s + 16, LANES), F32),
            pltpu.VMEM((s, LANES), F32),
            pltpu.VMEM((s, LANES), F32),
            pltpu.VMEM((s, LANES), F32),
            pltpu.VMEM((s, LANES), F32),
            pltpu.VMEM((2, GDN_DIM, GDN_DIM), F32),
            pltpu.VMEM((4, nch, GDN_DIM + GDN_CHUNK, GDN_DIM), BF16),
            pltpu.VMEM((4, nch, GDN_DIM, GDN_DIM), F32),
            pltpu.VMEM((4, GDN_BLOCK, GDN_DIM), F32),
            pltpu.VMEM((4, GDN_BLOCK, LANES), F32),
        ],
        compiler_params=_cparams("parallel", "arbitrary"),
        name="stage_c_gdn",
    )(proj3, proj3, proj3, proj3, prm["conv_w"], prm["conv_w"], prm["conv_w"], gcb3, gct4, prm["out_norm"])


def _merge_kernel(ya_ref, yb_ref, ga0_ref, ga1_ref, gb0_ref, gb1_ref, x_ref, wa_ref, wb_ref, wo_ref, o_ref):
    half = D_MODEL // 2
    pa = _dot(ya_ref[...], wa_ref[...])
    pb = _dot(yb_ref[...], wb_ref[...])
    m0 = ga0_ref[...].astype(F32) * pa[:, 0:half] + gb0_ref[...].astype(F32) * pb[:, 0:half]
    m1 = ga1_ref[...].astype(F32) * pa[:, half:] + gb1_ref[...].astype(F32) * pb[:, half:]
    mixed = jnp.concatenate([m0, m1], axis=1).astype(BF16)
    o_ref[...] = x_ref[...] + _dot(mixed, wo_ref[...])


def _stage_d(ya2d, yb2d, proj2d, x2d, prm, tm):
    t = x2d.shape[0]
    half = D_MODEL // 2
    gate0 = (COL_GZ + GDN_HEADS) * LANES // half
    row = lambda c: pl.BlockSpec((tm, half), lambda i: (i, c))
    return pl.pallas_call(
        _merge_kernel,
        grid=(t // tm,),
        in_specs=[
            row(0), row(0), row(gate0), row(gate0 + 1), row(gate0 + 2), row(gate0 + 3),
            pl.BlockSpec((tm, D_MODEL), lambda i: (i, 0)),
            _const_spec((half, D_MODEL)),
            _const_spec((half, D_MODEL)),
            _const_spec((D_MODEL, D_MODEL)),
        ],
        out_specs=pl.BlockSpec((tm, D_MODEL), lambda i: (i, 0)),
        out_shape=jax.ShapeDtypeStruct((t, D_MODEL), F32),
        compiler_params=_cparams("parallel"),
        name="stage_d_merge",
    )(ya2d, yb2d, proj2d, proj2d, proj2d, proj2d, x2d, prm["w_branch_a"], prm["w_branch_b"], prm["w_out"])


def _ffn_kernel(x_ref, g_ref, wu_ref, wg_ref, wo_ref, o_ref, hn_ref, act_ref, *, tf, tn, nf):
    j = pl.program_id(1)

    @pl.when(j == 0)
    def _():
        hn_ref[...] = _rmsnorm_rows(x_ref[...], g_ref[...]).astype(BF16)

    @pl.when(j < nf)
    def _():
        hn = hn_ref[...]
        up = _dot(hn, wu_ref[...])
        gt = _dot(hn, wg_ref[...])
        act_ref[:, pl.ds(pl.multiple_of(j * tf, tf), tf)] = (gt * _sigmoid(gt) * up).astype(BF16)

    @pl.when(j >= nf)
    def _():
        c0 = pl.multiple_of((j - nf) * tn, tn)
        o_ref[...] = x_ref[:, pl.ds(c0, tn)] + _dot(act_ref[...], wo_ref[...])


def _stage_e(x2d, prm, tm, tf, tn):
    t = x2d.shape[0]
    nf = FFN_HIDDEN // tf
    kern = functools.partial(_ffn_kernel, tf=tf, tn=tn, nf=nf)
    return pl.pallas_call(
        kern,
        grid=(t // tm, nf + D_MODEL // tn),
        in_specs=[
            pl.BlockSpec((tm, D_MODEL), lambda i, j: (i, 0)),
            _const_spec((1, D_MODEL)),
            pl.BlockSpec((D_MODEL, tf), lambda i, j: (0, jnp.minimum(j, nf - 1))),
            pl.BlockSpec((D_MODEL, tf), lambda i, j: (0, nf + jnp.minimum(j, nf - 1))),
            pl.BlockSpec((FFN_HIDDEN, tn), lambda i, j: (0, jnp.maximum(j - nf, 0))),
        ],
        out_specs=pl.BlockSpec((tm, tn), lambda i, j: (i, jnp.maximum(j - nf, 0))),
        out_shape=jax.ShapeDtypeStruct((t, D_MODEL), F32),
        scratch_shapes=[pltpu.VMEM((tm, D_MODEL), BF16), pltpu.VMEM((tm, FFN_HIDDEN), BF16)],
        compiler_params=_cparams("parallel", "arbitrary"),
        name="stage_e_ffn",
    )(x2d, prm["norm_ffn"], prm["w_ffn_in"], prm["w_ffn_in"], prm["w_ffn_out"])


def _ple_kernel(x_ref, g_ref, wg_ref, p_ref, wp_ref, o_ref):
    x = x_ref[...]
    hn = _rmsnorm_rows(x, g_ref[...]).astype(BF16)
    gate = _sigmoid(_dot(hn, wg_ref[...]))
    emb = _dot(p_ref[...].astype(BF16), wp_ref[...])
    o_ref[...] = x + gate * emb


def _stage_f(x2d, p2d, prm, tm):
    t = x2d.shape[0]
    return pl.pallas_call(
        _ple_kernel,
        grid=(t // tm,),
        in_specs=[
            pl.BlockSpec((tm, D_MODEL), lambda i: (i, 0)),
            _const_spec((1, D_MODEL)),
            _const_spec((D_MODEL, D_MODEL)),
            pl.BlockSpec((tm, PLE_DIM), lambda i: (i, 0)),
            _const_spec((PLE_DIM, D_MODEL)),
        ],
        out_specs=pl.BlockSpec((tm, D_MODEL), lambda i: (i, 0)),
        out_shape=jax.ShapeDtypeStruct((t, D_MODEL), F32),
        compiler_params=_cparams("parallel"),
        name="stage_f_ple",
    )(x2d, prm["norm_ple"], prm["w_ple_gate"], p2d, prm["w_ple_proj"])


def _block_diag_mask(n, blk, kind):
    r = jnp.arange(n)[:, None]
    c = jnp.arange(n)[None, :]
    same = (r // blk) == (c // blk)
    if kind == "lower":
        same = same & (c <= r)
    elif kind == "upper":
        same = same & (c >= r)
    return same.astype(BF16)


def _prepare_params(norm_mix, w_in, da_q_norm, da_k_norm, lambda_q1, lambda_k1, lambda_q2, lambda_k2,
                    da_subln, gdn_conv, gdn_a_log, gdn_dt_bias, gdn_out_norm, w_branch_a, w_branch_b,
                    w_out, norm_ffn, w_ffn_in, w_ffn_out, norm_ple, w_ple_gate, w_ple_proj):
    n_fixed = COL_GZ * LANES + GDN_HEADS * GDN_DIM
    n_ab = 4 * GDN_HEADS
    row = lambda v: v.reshape(1, -1).astype(F32)
    return {
        "norm_mix": row(norm_mix),
        "w_main": jnp.concatenate([w_in[:, :n_fixed], w_in[:, n_fixed + n_ab:]], axis=1).astype(BF16),
        "w_ab": jnp.pad(w_in[:, n_fixed:n_fixed + n_ab], ((0, 0), (0, LANES - n_ab))).astype(BF16),
        "grp64": _block_diag_mask(MXU_DIM, DA_HEAD_DIM, "full"),
        "q_gain": row(jnp.tile(da_q_norm, MXU_DIM // DA_HEAD_DIM)),
        "k_gain": row(jnp.tile(da_k_norm, MXU_DIM // DA_HEAD_DIM)),
        "slopes": (2.0 ** (-8.0 * jnp.arange(1, DA_HEADS + 1, dtype=F32) / DA_HEADS)) * LOG2E,
        "lam_params": jnp.stack([lambda_q1, lambda_k1, lambda_q2, lambda_k2]).astype(F32),
        "subln": row(da_subln),
        "conv_w": jnp.pad(gdn_conv.astype(F32), ((0, 8 - CONV_K), (0, 0))),
        "alog_row": jnp.pad(gdn_a_log.reshape(1, -1).astype(F32), ((0, 0), (0, LANES - 2 * GDN_HEADS))),
        "dtb_row": jnp.pad(gdn_dt_bias.reshape(1, -1).astype(F32), ((0, 0), (0, LANES - 2 * GDN_HEADS))),
        "ltri": _block_diag_mask(GDN_BLOCK, GDN_CHUNK, "lower"),
        "utri": _block_diag_mask(GDN_BLOCK, GDN_CHUNK, "upper"),
        "ones_bd": _block_diag_mask(GDN_BLOCK, GDN_CHUNK, "full"),
        "out_norm": row(gdn_out_norm),
        "w_branch_a": w_branch_a.astype(BF16),
        "w_branch_b": w_branch_b.astype(BF16),
        "w_out": w_out.astype(BF16),
        "norm_ffn": row(norm_ffn),
        "w_ffn_in": w_ffn_in.astype(BF16),
        "w_ffn_out": w_ffn_out.astype(BF16),
        "norm_ple": row(norm_ple),
        "w_ple_gate": w_ple_gate.astype(BF16),
        "w_ple_proj": w_ple_proj.astype(BF16),
    }


def _tile(n, pref):
    return pref if n % pref == 0 else n


def _encoder_layer(x, p, prm):
    b, s, _ = x.shape
    t = b * s
    x2d = x.reshape(t, D_MODEL)
    proj, ab = _stage_a(x2d, prm, _tile(t, 1024))
    proj3 = proj.reshape(b, s, PROJ_MAIN_W)
    ya = _stage_b(proj3, prm, _tile(s, 512), _tile(s, 512))
    gcb, gct = _stage_c0(ab, prm)
    yb = _stage_c(proj3, gcb.reshape(b, s, LANES), gct.reshape(b, s // GDN_BLOCK, LANES, GDN_BLOCK), prm)
    x1 = _stage_d(ya.reshape(t, -1), yb.reshape(t, -1), proj, x2d, prm, _tile(t, 512))
    x2 = _stage_e(x1, prm, _tile(t, 1024), 512, 256)
    x3 = _stage_f(x2, p.reshape(t, PLE_DIM), prm, _tile(t, 512))
    return x3.reshape(b, s, D_MODEL)


def kernel(x_prompt, x_sample, p_prompt, p_sample, norm_mix, w_in, da_q_norm, da_k_norm, lambda_q1, lambda_k1, lambda_q2, lambda_k2, da_subln, gdn_conv, gdn_a_log, gdn_dt_bias, gdn_out_norm, w_branch_a, w_branch_b, w_out, norm_ffn, w_ffn_in, w_ffn_out, norm_ple, w_ple_gate, w_ple_proj):
    layer_params = (norm_mix, w_in, da_q_norm, da_k_norm, lambda_q1, lambda_k1, lambda_q2, lambda_k2,
                    da_subln, gdn_conv, gdn_a_log, gdn_dt_bias, gdn_out_norm, w_branch_a, w_branch_b,
                    w_out, norm_ffn, w_ffn_in, w_ffn_out, norm_ple, w_ple_gate, w_ple_proj)
    depth = norm_mix.shape[0]
    assert depth == 1, "LAM_INIT is the depth-0 value"
    y_prompt, y_sample = x_prompt, x_sample
    for layer in range(depth):
        prm = _prepare_params(*(w[layer] for w in layer_params))
        y_prompt = _encoder_layer(y_prompt, p_prompt[layer], prm)
        y_sample = _encoder_layer(y_sample, p_sample[layer], prm)
    return (y_prompt, y_sample)
```

```python
import functools
import math

import jax
import jax.numpy as jnp
from jax import lax
from jax.experimental import pallas as pl
from jax.experimental.pallas import tpu as pltpu

F32 = jnp.float32
BF16 = jnp.bfloat16

D_MODEL = 2048
DA_HEADS = 8
DA_HEAD_DIM = 64
GDN_HEADS = 8
GDN_DIM = 128
GDN_CHUNK = 64
CONV_K = 5
FFN_HIDDEN = 5632
PLE_DIM = 256
NORM_EPS = 1e-6
LAM_INIT = 0.8 - 0.6 * math.exp(-0.3 * 0)
LOG2E = math.log2(math.e)

LANES = 128
MXU_DIM = 256
GDN_BLOCK = 256
VMEM_LIMIT = 56 * 1024 * 1024
BOUND_SLACK = 1.01
SKIP_LOG2 = 160.0

COL_Q, COL_K, COL_V = 0, 8, 16
COL_GQ, COL_GK, COL_GV, COL_GZ = 24, 32, 40, 48
PROJ_MAIN_W = 11264
PROJ_TN = 1024

NT_DIMS = (((1,), (1,)), ((), ()))
TN_DIMS = (((0,), (0,)), ((), ()))


def _cparams(*sem):
    return pltpu.CompilerParams(dimension_semantics=sem, vmem_limit_bytes=VMEM_LIMIT)


def _const_spec(shape):
    nd = len(shape)
    return pl.BlockSpec(shape, lambda *_: (0,) * nd, pipeline_mode=pl.Buffered(1))


def _rmsnorm_rows(x, gain):
    ms = jnp.mean(x * x, axis=-1, keepdims=True)
    return x * lax.rsqrt(ms + NORM_EPS) * gain


def _sigmoid(x):
    return 1.0 / (1.0 + jnp.exp(-x))


def _split3(x):
    hi = x.astype(BF16)
    r1 = x - hi.astype(F32)
    mid = r1.astype(BF16)
    lo = (r1 - mid.astype(F32)).astype(BF16)
    return hi, mid, lo


def _dot(a, b):
    return jnp.dot(a, b, preferred_element_type=F32)


def _proj_kernel(x_ref, g_ref, w_ref, wab_ref, grp_ref, qg_ref, kg_ref, o_ref, ab_ref, hn_ref):
    j = pl.program_id(1)

    @pl.when(j == 0)
    def _():
        hb = _rmsnorm_rows(x_ref[...], g_ref[...]).astype(BF16)
        hn_ref[...] = hb
        ab_ref[...] = _dot(hb, wab_ref[...])

    @pl.when(j < 2)
    def _():
        acc = _dot(hn_ref[...], w_ref[...])
        gain = jnp.where(j == 0, qg_ref[...] * (DA_HEAD_DIM ** -0.5 * LOG2E), kg_ref[...])
        for c in range(PROJ_TN // MXU_DIM):
            a = acc[:, c * MXU_DIM:(c + 1) * MXU_DIM]
            sq = a * a
            hi = sq.astype(BF16)
            lo = (sq - hi.astype(F32)).astype(BF16)
            ss = _dot(hi, grp_ref[...]) + _dot(lo, grp_ref[...])
            y = a * lax.rsqrt(ss * (1.0 / DA_HEAD_DIM) + NORM_EPS) * gain
            o_ref[:, c * MXU_DIM:(c + 1) * MXU_DIM] = y.astype(BF16)

    @pl.when(j >= 2)
    def _():
        acc = _dot(hn_ref[...], w_ref[...])
        o_ref[...] = jnp.where(j >= 7, _sigmoid(acc), acc).astype(BF16)


def _stage_a(x2d, prm, tm):
    t = x2d.shape[0]
    return pl.pallas_call(
        _proj_kernel,
        grid=(t // tm, PROJ_MAIN_W // PROJ_TN),
        in_specs=[
            pl.BlockSpec((tm, D_MODEL), lambda i, j: (i, 0)),
            _const_spec((1, D_MODEL)),
            pl.BlockSpec((D_MODEL, PROJ_TN), lambda i, j: (0, j)),
            _const_spec((D_MODEL, LANES)),
            _const_spec((MXU_DIM, MXU_DIM)),
            _const_spec((1, MXU_DIM)),
            _const_spec((1, MXU_DIM)),
        ],
        out_specs=[
            pl.BlockSpec((tm, PROJ_TN), lambda i, j: (i, j)),
            pl.BlockSpec((tm, LANES), lambda i, j: (i, 0)),
        ],
        out_shape=[
            jax.ShapeDtypeStruct((t, PROJ_MAIN_W), BF16),
            jax.ShapeDtypeStruct((t, LANES), F32),
        ],
        scratch_shapes=[pltpu.VMEM((tm, D_MODEL), BF16)],
        compiler_params=_cparams("parallel", "arbitrary"),
        name="stage_a_proj",
    )(x2d, prm["norm_mix"], prm["w_main"], prm["w_ab"], prm["grp64"], prm["q_gain"], prm["k_gain"])


def _attn_kernel(slopes_ref, q_ref, k_ref, v_ref, lam_ref, subln_ref, qg_ref, kg_ref, o_ref,
                 kx_ref, vx_ref, qx_ref, sa_ref, sb_ref, m_ref, acc_ref, *, tq, tk, seq):
    h = pl.program_id(1)
    qi = pl.program_id(2)
    slope = slopes_ref[h]

    def split3_f32(x):
        return tuple(p.astype(F32) for p in _split3(x))

    def ext_lanes(shape, pieces, unit):
        lane = lax.broadcasted_iota(jnp.int32, shape, 1)
        hi, mid, lo = pieces
        return jnp.where(lane == 0, hi, jnp.where(lane == 1, mid, jnp.where(lane == 2, lo,
                         jnp.where(lane < 6, unit, 0.0)))).astype(BF16)

    @pl.when(qi == 0)
    def _():
        def fill(i, c):
            r0 = pl.multiple_of(i * tk, tk)
            pos = (r0 + lax.broadcasted_iota(jnp.int32, (tk, LANES), 0)).astype(F32)
            hi, mid, lo = split3_f32(slope * pos)
            lane = lax.broadcasted_iota(jnp.int32, (tk, LANES), 1)
            kext = jnp.where(lane < 3, 1.0, jnp.where(lane == 3, hi, jnp.where(lane == 4, mid,
                             jnp.where(lane == 5, lo, 0.0)))).astype(BF16)
            kx_ref[pl.ds(r0, tk), 0:LANES] = k_ref[0, pl.ds(r0, tk), :]
            kx_ref[pl.ds(r0, tk), LANES:2 * LANES] = kext
            vx_ref[pl.ds(r0, tk), 0:LANES] = v_ref[0, pl.ds(r0, tk), :]
            vx_ref[pl.ds(r0, tk), LANES:2 * LANES] = jnp.ones((tk, LANES), BF16)
            return c

        lax.fori_loop(0, seq // tk, fill, 0)

    q0 = qi * tq
    q = q_ref[0]
    lane = lax.broadcasted_iota(jnp.int32, (tq, LANES), 1)
    qm0 = jnp.where(lane < DA_HEAD_DIM, q, jnp.zeros_like(q))
    qm1 = jnp.where(lane >= DA_HEAD_DIM, q, jnp.zeros_like(q))
    pos = (q0 + lax.broadcasted_iota(jnp.int32, (tq, LANES), 0)).astype(F32)
    hi, mid, lo = split3_f32(slope * pos)
    ext_l = ext_lanes((tq, LANES), (-hi, -mid, -lo), 1.0)
    ext_r = ext_lanes((tq, LANES), (hi, mid, lo), -1.0)
    for side, ext in ((0, ext_l), (1, ext_r)):
        qx_ref[side, 0:tq, 0:LANES] = qm0
        qx_ref[side, tq:2 * tq, 0:LANES] = qm1
        qx_ref[side, 0:tq, LANES:2 * LANES] = ext
        qx_ref[side, tq:2 * tq, LANES:2 * LANES] = ext
    m_ref[...] = jnp.full(m_ref.shape, -1e30, F32)
    acc_ref[...] = jnp.zeros(acc_ref.shape, F32)

    nk = seq // tk
    n_left = q0 // tk

    gq = jnp.max(jnp.abs(qg_ref[...]), axis=-1, keepdims=True)
    gk = jnp.max(jnp.abs(kg_ref[...]), axis=-1, keepdims=True)
    bound = (DA_HEAD_DIM * DA_HEAD_DIM ** -0.5 * LOG2E * BOUND_SLACK) * gq * gk
    w_f = jnp.minimum((SKIP_LOG2 + 2.0 * bound) / slope, float(seq))
    w = jnp.ceil(w_f).astype(jnp.int32)[0, 0]
    lo = jnp.minimum(jnp.maximum(q0 - w, 0) // tk, n_left)
    hi = jnp.minimum((w + q0 + tq - 2) // tk + 1, nk)
    odd = (hi - lo) % 2 == 1
    hi = jnp.where(odd & (lo == 0), hi + 1, hi)
    lo = jnp.where(odd & (lo > 0), lo - 1, lo)
    n_pairs = (hi - lo) // 2

    def kblock(ki):
        return kx_ref[pl.ds(pl.multiple_of(ki * tk, tk), tk), :]

    def scores(ki):
        side = jnp.where(ki >= n_left, 1, 0)
        return lax.dot_general(qx_ref[side], kblock(ki), NT_DIMS, preferred_element_type=F32)

    def consume(s, ki):
        m_old = m_ref[...]
        m_new = jnp.maximum(m_old, jnp.max(s, axis=-1, keepdims=True))
        alpha = jnp.exp2(m_old - m_new)
        p = jnp.exp2(s - jnp.concatenate([m_new] * (tk // LANES), axis=1))
        vx = vx_ref[pl.ds(pl.multiple_of(ki * tk, tk), tk), :]
        pv = _dot(p.astype(BF16), vx)
        acc_ref[...] = jnp.concatenate([alpha, alpha], axis=1) * acc_ref[...] + pv
        m_ref[...] = m_new

    def nth(j):
        i = lo + j - 1
        return jnp.where(i >= n_left, i + 1, i)

    kd = kblock(n_left)
    sa_ref[...] = jnp.minimum(lax.dot_general(qx_ref[0], kd, NT_DIMS, preferred_element_type=F32),
                              lax.dot_general(qx_ref[1], kd, NT_DIMS, preferred_element_type=F32))
    if nk == 1:
        consume(sa_ref[...], n_left)
    else:
        def pair(t, last):
            first = jnp.where(t == 0, n_left, nth(2 * t))
            sb_ref[...] = scores(nth(2 * t + 1))
            consume(sa_ref[...], first)
            if not last:
                sa_ref[...] = scores(nth(2 * t + 2))
            consume(sb_ref[...], nth(2 * t + 1))

        def body(t, carry):
            pair(t, False)
            return carry

        lax.fori_loop(0, n_pairs - 1, body, 0)
        pair(n_pairs - 1, True)

    acc = acc_ref[...]
    o = acc[:, 0:LANES] * (1.0 / acc[:, LANES:2 * LANES])
    lp = lam_ref[...]
    lam = (jnp.exp(jnp.sum(lp[0:1, :] * lp[1:2, :], axis=-1, keepdims=True))
           - jnp.exp(jnp.sum(lp[2:3, :] * lp[3:4, :], axis=-1, keepdims=True)) + LAM_INIT)
    d = o[0:tq, :] - lam * o[tq:2 * tq, :]
    y = _rmsnorm_rows(d, subln_ref[...]) * (1.0 - LAM_INIT)
    o_ref[0] = y.astype(BF16)


def _stage_b(proj3, prm, tq, tk):
    b, s, _ = proj3.shape
    assert tk % tq == 0 and s % tk == 0, "each query block must sit inside one key block"
    assert (s // tk) % 2 == 0 or s == tk, "key blocks are consumed in pairs"
    kern = functools.partial(_attn_kernel, tq=tq, tk=tk, seq=s)
    return pl.pallas_call(
        kern,
        grid=(b, DA_HEADS, s // tq),
        in_specs=[
            pl.BlockSpec(memory_space=pltpu.SMEM),
            pl.BlockSpec((1, tq, LANES), lambda bi, h, qi: (bi, qi, COL_Q + h)),
            pl.BlockSpec((1, s, LANES), lambda bi, h, qi: (bi, 0, COL_K + h)),
            pl.BlockSpec((1, s, LANES), lambda bi, h, qi: (bi, 0, COL_V + h)),
            _const_spec((4, DA_HEAD_DIM)),
            _const_spec((1, LANES)),
            _const_spec((1, MXU_DIM)),
            _const_spec((1, MXU_DIM)),
        ],
        out_specs=pl.BlockSpec((1, tq, LANES), lambda bi, h, qi: (bi, qi, h)),
        out_shape=jax.ShapeDtypeStruct((b, s, DA_HEADS * LANES), BF16),
        scratch_shapes=[
            pltpu.VMEM((s, 2 * LANES), BF16),
            pltpu.VMEM((s, 2 * LANES), BF16),
            pltpu.VMEM((2, 2 * tq, 2 * LANES), BF16),
            pltpu.VMEM((2 * tq, tk), F32),
            pltpu.VMEM((2 * tq, tk), F32),
            pltpu.VMEM((2 * tq, LANES), F32),
            pltpu.VMEM((2 * tq, 2 * LANES), F32),
        ],
        compiler_params=_cparams("parallel", "parallel", "arbitrary"),
        name="stage_b_diff_attn",
    )(prm["slopes"], proj3, proj3, proj3, prm["lam_params"], prm["subln"], prm["q_gain"], prm["k_gain"])


def _softplus(x):
    return jnp.maximum(x, 0.0) + jnp.log1p(jnp.exp(-jnp.abs(x)))


def _gdn_gate_kernel(ab_ref, alog_ref, dtb_ref, ltri_ref, utri_ref, ones_ref, o_ref, ot_ref):
    x = ab_ref[...]
    g = -jnp.exp(alog_ref[...]) * _softplus(x + dtb_ref[...])
    parts = _split3(g)
    cf = sum(_dot(ltri_ref[...], p) for p in parts)
    cb = sum(_dot(utri_ref[...], p) for p in parts)
    tot = sum(_dot(ones_ref[...], p) for p in parts)
    lane = lax.broadcasted_iota(jnp.int32, x.shape, 1)
    out = jnp.where(lane < 8, cf,
                    jnp.where(lane < 16, cb,
                              jnp.where(lane < 32, _sigmoid(x), pltpu.roll(tot, 32, axis=1))))
    o_ref[...] = out
    ot_ref[0] = out.T


def _stage_c0(ab2d, prm):
    t = ab2d.shape[0]
    nb = t // GDN_BLOCK
    return pl.pallas_call(
        _gdn_gate_kernel,
        grid=(nb,),
        in_specs=[
            pl.BlockSpec((GDN_BLOCK, LANES), lambda i: (i, 0)),
            _const_spec((1, LANES)),
            _const_spec((1, LANES)),
            _const_spec((GDN_BLOCK, GDN_BLOCK)),
            _const_spec((GDN_BLOCK, GDN_BLOCK)),
            _const_spec((GDN_BLOCK, GDN_BLOCK)),
        ],
        out_specs=[
            pl.BlockSpec((GDN_BLOCK, LANES), lambda i: (i, 0)),
            pl.BlockSpec((1, LANES, GDN_BLOCK), lambda i: (i, 0, 0)),
        ],
        out_shape=[
            jax.ShapeDtypeStruct((t, LANES), F32),
            jax.ShapeDtypeStruct((nb, LANES, GDN_BLOCK), F32),
        ],
        compiler_params=_cparams("parallel"),
        name="stage_c0_gdn_gates",
    )(ab2d, prm["alog_row"], prm["dtb_row"], prm["ltri"], prm["utri"], prm["ones_bd"])


def _gdn_kernel(qp_ref, kp_ref, vp_ref, z_ref, wq_ref, wk_ref, wv_ref, gcb_ref, gct_ref, onorm_ref,
                o_ref, xp_ref, qn_ref, kn_ref, vn_ref, of_ref, st_ref, xs_ref, bs_ref, oi_ref, eg_ref, *, seq):
    h = pl.program_id(1)
    nblk = seq // GDN_BLOCK
    nch = GDN_BLOCK // GDN_CHUNK
    pad = 8
    rows = 2 * GDN_BLOCK

    def conv_phase(src_ref, w_ref, dst_ref, mode):
        xp_ref[0:pad, :] = jnp.zeros((pad, LANES), F32)
        xp_ref[pad + seq:pad + seq + pad, :] = jnp.zeros((pad, LANES), F32)

        def fill(i, c):
            r0 = pl.multiple_of(i * rows, rows)
            xp_ref[pl.ds(pad + r0, rows), :] = src_ref[0, pl.ds(r0, rows), :].astype(F32)
            return c

        lax.fori_loop(0, seq // rows, fill, 0)

        def conv(i, c):
            r0 = pl.multiple_of(i * rows, rows)
            acc = jnp.zeros((rows, LANES), F32)
            for t in range(CONV_K):
                acc = acc + w_ref[t:t + 1, :] * xp_ref[pl.ds(r0 + pad + t - (CONV_K - 1) // 2, rows), :]
            y = acc * _sigmoid(acc)
            if mode != "v":
                y = y * lax.rsqrt(jnp.sum(y * y, axis=-1, keepdims=True) + NORM_EPS)
            if mode == "q":
                y = y * (GDN_DIM ** -0.5)
            dst_ref[pl.ds(r0, rows), :] = y
            return c

        lax.fori_loop(0, seq // rows, conv, 0)

    conv_phase(qp_ref, wq_ref, qn_ref, "q")
    conv_phase(kp_ref, wk_ref, kn_ref, "k")
    conv_phase(vp_ref, wv_ref, vn_ref, "v")

    st_ref[...] = jnp.zeros(st_ref.shape, F32)

    ri = lax.broadcasted_iota(jnp.int32, (GDN_BLOCK, GDN_BLOCK), 0)
    ci = lax.broadcasted_iota(jnp.int32, (GDN_BLOCK, GDN_BLOCK), 1)
    same = (ri // GDN_CHUNK) == (ci // GDN_CHUNK)
    eye = (ri == ci).astype(F32)
    lane = lax.broadcasted_iota(jnp.int32, (GDN_BLOCK, LANES), 1)

    def pick(gblk, col):
        return jnp.sum(jnp.where(lane == col, gblk, 0.0), axis=-1, keepdims=True)

    def prep(blk, direction, slot):
        r0 = pl.multiple_of(blk * GDN_BLOCK, GDN_BLOCK)
        q = qn_ref[pl.ds(r0, GDN_BLOCK), :]
        k = kn_ref[pl.ds(r0, GDN_BLOCK), :]
        v = vn_ref[pl.ds(r0, GDN_BLOCK), :]
        gblk = gcb_ref[0, pl.ds(r0, GDN_BLOCK), :]
        col = direction * GDN_HEADS + h
        gcol = pick(gblk, col)
        bcol = pick(gblk, 16 + col)
        tcol = pick(gblk, 32 + col)
        grow = gct_ref[0, blk, pl.ds(col, 1), :]
        if direction == 0:
            incl, strict = same & (ci <= ri), same & (ci < ri)
        else:
            incl, strict = same & (ci >= ri), same & (ci > ri)
        decay = jnp.where(incl, jnp.exp(jnp.where(incl, gcol - grow, 0.0)), 0.0)

        kbf = k.astype(BF16)
        kb = k * bcol
        eg = jnp.exp(gcol)
        rhs = jnp.concatenate([v * bcol, kb * eg], axis=1)
        kdec = (k * jnp.exp(tcol - gcol)).astype(BF16)
        egl = jnp.broadcast_to(jnp.exp(tcol), (GDN_BLOCK, LANES))
        kq = lax.dot_general(jnp.concatenate([kb, q], axis=0).astype(BF16), kbf, NT_DIMS,
                             preferred_element_type=F32)
        yield
        a = jnp.where(strict, kq[0:GDN_BLOCK, :] * decay, 0.0)
        qkm = (kq[GDN_BLOCK:, :] * decay).astype(BF16)
        assert GDN_CHUNK == 64
        a1 = a.astype(BF16)
        ima = eye - a
        a2f = _dot(a1, a1)
        yield
        a2 = a2f.astype(BF16)
        a4f = _dot(a2, a2)
        p1 = ima + _dot(ima.astype(BF16), a2)
        yield
        a4 = a4f.astype(BF16)
        a8f = _dot(a4, a4)
        yield
        a8 = a8f.astype(BF16)
        p2 = eye + a4f + a8f + _dot(a4, a8)
        a16 = _dot(a8, a8).astype(BF16)
        yield
        a32 = _dot(a16, a16).astype(BF16)
        p12 = _dot(p1.astype(BF16), p2.astype(BF16))
        x1 = rhs + _dot(a16, rhs.astype(BF16))
        yield
        x2 = x1 + _dot(a32, x1.astype(BF16))
        yield
        solb = _dot(p12.astype(BF16), x2.astype(BF16)).astype(BF16)
        yield
        intra = _dot(qkm, solb)
        bks = [lax.dot_general(kdec[c * GDN_CHUNK:(c + 1) * GDN_CHUNK, :], solb[c * GDN_CHUNK:(c + 1) * GDN_CHUNK, :],
                               TN_DIMS, preferred_element_type=F32) for c in range(nch)]
        yield
        eg_ref[slot] = egl
        oi_ref[slot] = intra[:, 0:GDN_DIM]
        qt = (q * eg - intra[:, GDN_DIM:]).astype(BF16)
        for c in range(nch):
            xs_ref[slot, c, 0:GDN_DIM, :] = bks[c][:, GDN_DIM:].astype(BF16)
            xs_ref[slot, c, GDN_DIM:GDN_DIM + GDN_CHUNK, :] = qt[c * GDN_CHUNK:(c + 1) * GDN_CHUNK, :]
            bs_ref[slot, c] = bks[c][:, 0:GDN_DIM]

    def recur_chunk(blk, direction, slot, c, out_ref):
        state = st_ref[direction]
        y = _dot(xs_ref[slot, c], state.astype(BF16))
        r0 = pl.multiple_of(blk * GDN_BLOCK, GDN_BLOCK)
        out_ref[pl.ds(r0 + c * GDN_CHUNK, GDN_CHUNK), :] = y[GDN_DIM:, :] + oi_ref[slot, c * GDN_CHUNK:(c + 1) * GDN_CHUNK, :]
        st_ref[direction] = state * eg_ref[slot, c * GDN_CHUNK:c * GDN_CHUNK + 1, :] + bs_ref[slot, c] - y[0:GDN_DIM, :]

    def step_blocks(j):
        return ((2 * j, 0, 0), (2 * j + 1, 0, 1), (nblk - 1 - 2 * j, 1, 2), (nblk - 2 - 2 * j, 1, 3))

    def recur_steps(j):
        (f0, _, _), (f1, _, _), (b0, _, _), (b1, _, _) = step_blocks(j)
        for fb, bb, fs, bs in ((f0, b0, 0, 2), (f1, b1, 1, 3)):
            for c in range(nch):
                yield lambda fb=fb, fs=fs, c=c: recur_chunk(fb, 0, fs, c, of_ref)
                yield lambda bb=bb, bs=bs, c=c: recur_chunk(bb, 1, bs, nch - 1 - c, xp_ref)

    def emit(prep_step, recur_step):
        gens = [prep(*b) for b in step_blocks(prep_step)] if prep_step is not None else []
        rec = list(recur_steps(recur_step)) if recur_step is not None else []
        n_levels = 9
        per_level = -(-len(rec) // (n_levels - 1)) if gens else len(rec)
        live = True
        while live or rec:
            live = False
            for g in gens:
                try:
                    next(g)
                    live = True
                except StopIteration:
                    pass
            last_level = bool(gens) and not live
            if not last_level:
                for f in rec[:per_level]:
                    f()
                rec = rec[per_level:]
            assert not (last_level and rec), "recurrence reads must all precede the scratch overwrite"

    nstep = nblk // 2
    emit(0, None)

    def main(j, c):
        emit(j, j - 1)
        return c

    lax.fori_loop(1, nstep, main, 0)
    emit(None, nstep - 1)

    def fin(i, c):
        r0 = pl.multiple_of(i * rows, rows)
        o = of_ref[pl.ds(r0, rows), :] + xp_ref[pl.ds(r0, rows), :]
        z = z_ref[0, pl.ds(r0, rows), :].astype(F32)
        y = _rmsnorm_rows(o, onorm_ref[...]) * (z * _sigmoid(z))
        o_ref[0, pl.ds(r0, rows), :] = y.astype(BF16)
        return c

    lax.fori_loop(0, seq // rows, fin, 0)


def _stage_c(proj3, gcb3, gct4, prm):
    b, s, _ = proj3.shape
    nblk = s // GDN_BLOCK
    nch = GDN_BLOCK // GDN_CHUNK
    assert nblk % 2 == 0

    def once(shape, imap):
        return pl.BlockSpec(shape, imap, pipeline_mode=pl.Buffered(1))

    kern = functools.partial(_gdn_kernel, seq=s)
    return pl.pallas_call(
        kern,
        grid=(b, GDN_HEADS),
        in_specs=[
            once((1, s, LANES), lambda bi, h: (bi, 0, COL_GQ + h)),
            once((1, s, LANES), lambda bi, h: (bi, 0, COL_GK + h)),
            once((1, s, LANES), lambda bi, h: (bi, 0, COL_GV + h)),
            once((1, s, LANES), lambda bi, h: (bi, 0, COL_GZ + h)),
            pl.BlockSpec((8, LANES), lambda bi, h: (0, h)),
            pl.BlockSpec((8, LANES), lambda bi, h: (0, GDN_HEADS + h)),
            pl.BlockSpec((8, LANES), lambda bi, h: (0, 2 * GDN_HEADS + h)),
            once((1, s, LANES), lambda bi, h: (bi, 0, 0)),
            once((1, nblk, 2 * GDN_HEADS, GDN_BLOCK), lambda bi, h: (bi, 0, 0, 0)),
            _const_spec((1, LANES)),
        ],
        out_specs=pl.BlockSpec((1, s, LANES), lambda bi, h: (bi, 0, h)),
        out_shape=jax.ShapeDtypeStruct((b, s, GDN_HEADS * GDN_DIM), BF16),
        scratch_shapes=[
            pltpu.VMEM((s + 16, LANES), F32),
            pltpu.VMEM((s, LANES), F32),
            pltpu.VMEM((s, LANES), F32),
            pltpu.VMEM((s, LANES), F32),
            pltpu.VMEM((s, LANES), F32),
            pltpu.VMEM((2, GDN_DIM, GDN_DIM), F32),
            pltpu.VMEM((4, nch, GDN_DIM + GDN_CHUNK, GDN_DIM), BF16),
            pltpu.VMEM((4, nch, GDN_DIM, GDN_DIM), F32),
            pltpu.VMEM((4, GDN_BLOCK, GDN_DIM), F32),
            pltpu.VMEM((4, GDN_BLOCK, LANES), F32),
        ],
        compiler_params=_cparams("parallel", "arbitrary"),
        name="stage_c_gdn",
    )(proj3, proj3, proj3, proj3, prm["conv_w"], prm["conv_w"], prm["conv_w"], gcb3, gct4, prm["out_norm"])


def _merge_kernel(ya_ref, yb_ref, ga0_ref, ga1_ref, gb0_ref, gb1_ref, x_ref, wa_ref, wb_ref, wo_ref, o_ref):
    half = D_MODEL // 2
    pa = _dot(ya_ref[...], wa_ref[...])
    pb = _dot(yb_ref[...], wb_ref[...])
    m0 = ga0_ref[...].astype(F32) * pa[:, 0:half] + gb0_ref[...].astype(F32) * pb[:, 0:half]
    m1 = ga1_ref[...].astype(F32) * pa[:, half:] + gb1_ref[...].astype(F32) * pb[:, half:]
    mixed = jnp.concatenate([m0, m1], axis=1).astype(BF16)
    o_ref[...] = x_ref[...] + _dot(mixed, wo_ref[...])


def _stage_d(ya2d, yb2d, proj2d, x2d, prm, tm):
    t = x2d.shape[0]
    half = D_MODEL // 2
    gate0 = (COL_GZ + GDN_HEADS) * LANES // half
    row = lambda c: pl.BlockSpec((tm, half), lambda i: (i, c))
    return pl.pallas_call(
        _merge_kernel,
        grid=(t // tm,),
        in_specs=[
            row(0), row(0), row(gate0), row(gate0 + 1), row(gate0 + 2), row(gate0 + 3),
            pl.BlockSpec((tm, D_MODEL), lambda i: (i, 0)),
            _const_spec((half, D_MODEL)),
            _const_spec((half, D_MODEL)),
            _const_spec((D_MODEL, D_MODEL)),
        ],
        out_specs=pl.BlockSpec((tm, D_MODEL), lambda i: (i, 0)),
        out_shape=jax.ShapeDtypeStruct((t, D_MODEL), F32),
        compiler_params=_cparams("parallel"),
        name="stage_d_merge",
    )(ya2d, yb2d, proj2d, proj2d, proj2d, proj2d, x2d, prm["w_branch_a"], prm["w_branch_b"], prm["w_out"])


def _ffn_kernel(x_ref, g_ref, wu_ref, wg_ref, wo_ref, o_ref, hn_ref, act_ref, *, tf, tn, nf):
    j = pl.program_id(1)

    @pl.when(j == 0)
    def _():
        hn_ref[...] = _rmsnorm_rows(x_ref[...], g_ref[...]).astype(BF16)

    @pl.when(j < nf)
    def _():
        hn = hn_ref[...]
        up = _dot(hn, wu_ref[...])
        gt = _dot(hn, wg_ref[...])
        act_ref[:, pl.ds(pl.multiple_of(j * tf, tf), tf)] = (gt * _sigmoid(gt) * up).astype(BF16)

    @pl.when(j >= nf)
    def _():
        c0 = pl.multiple_of((j - nf) * tn, tn)
        o_ref[...] = x_ref[:, pl.ds(c0, tn)] + _dot(act_ref[...], wo_ref[...])


def _stage_e(x2d, prm, tm, tf, tn):
    t = x2d.shape[0]
    nf = FFN_HIDDEN // tf
    kern = functools.partial(_ffn_kernel, tf=tf, tn=tn, nf=nf)
    return pl.pallas_call(
        kern,
        grid=(t // tm, nf + D_MODEL // tn),
        in_specs=[
            pl.BlockSpec((tm, D_MODEL), lambda i, j: (i, 0)),
            _const_spec((1, D_MODEL)),
            pl.BlockSpec((D_MODEL, tf), lambda i, j: (0, jnp.minimum(j, nf - 1))),
            pl.BlockSpec((D_MODEL, tf), lambda i, j: (0, nf + jnp.minimum(j, nf - 1))),
            pl.BlockSpec((FFN_HIDDEN, tn), lambda i, j: (0, jnp.maximum(j - nf, 0))),
        ],
        out_specs=pl.BlockSpec((tm, tn), lambda i, j: (i, jnp.maximum(j - nf, 0))),
        out_shape=jax.ShapeDtypeStruct((t, D_MODEL), F32),
        scratch_shapes=[pltpu.VMEM((tm, D_MODEL), BF16), pltpu.VMEM((tm, FFN_HIDDEN), BF16)],
        compiler_params=_cparams("parallel", "arbitrary"),
        name="stage_e_ffn",
    )(x2d, prm["norm_ffn"], prm["w_ffn_in"], prm["w_ffn_in"], prm["w_ffn_out"])


def _ple_kernel(x_ref, g_ref, wg_ref, p_ref, wp_ref, o_ref):
    x = x_ref[...]
    hn = _rmsnorm_rows(x, g_ref[...]).astype(BF16)
    gate = _sigmoid(_dot(hn, wg_ref[...]))
    emb = _dot(p_ref[...].astype(BF16), wp_ref[...])
    o_ref[...] = x + gate * emb


def _stage_f(x2d, p2d, prm, tm):
    t = x2d.shape[0]
    return pl.pallas_call(
        _ple_kernel,
        grid=(t // tm,),
        in_specs=[
            pl.BlockSpec((tm, D_MODEL), lambda i: (i, 0)),
            _const_spec((1, D_MODEL)),
            _const_spec((D_MODEL, D_MODEL)),
            pl.BlockSpec((tm, PLE_DIM), lambda i: (i, 0)),
            _const_spec((PLE_DIM, D_MODEL)),
        ],
        out_specs=pl.BlockSpec((tm, D_MODEL), lambda i: (i, 0)),
        out_shape=jax.ShapeDtypeStruct((t, D_MODEL), F32),
        compiler_params=_cparams("parallel"),
        name="stage_f_ple",
    )(x2d, prm["norm_ple"], prm["w_ple_gate"], p2d, prm["w_ple_proj"])


def _block_diag_mask(n, blk, kind):
    r = jnp.arange(n)[:, None]
    c = jnp.arange(n)[None, :]
    same = (r // blk) == (c // blk)
    if kind == "lower":
        same = same & (c <= r)
    elif kind == "upper":
        same = same & (c >= r)
    return same.astype(BF16)


def _prepare_params(norm_mix, w_in, da_q_norm, da_k_norm, lambda_q1, lambda_k1, lambda_q2, lambda_k2,
                    da_subln, gdn_conv, gdn_a_log, gdn_dt_bias, gdn_out_norm, w_branch_a, w_branch_b,
                    w_out, norm_ffn, w_ffn_in, w_ffn_out, norm_ple, w_ple_gate, w_ple_proj):
    n_fixed = COL_GZ * LANES + GDN_HEADS * GDN_DIM
    n_ab = 4 * GDN_HEADS
    row = lambda v: v.reshape(1, -1).astype(F32)
    return {
        "norm_mix": row(norm_mix),
        "w_main": jnp.concatenate([w_in[:, :n_fixed], w_in[:, n_fixed + n_ab:]], axis=1).astype(BF16),
        "w_ab": jnp.pad(w_in[:, n_fixed:n_fixed + n_ab], ((0, 0), (0, LANES - n_ab))).astype(BF16),
        "grp64": _block_diag_mask(MXU_DIM, DA_HEAD_DIM, "full"),
        "q_gain": row(jnp.tile(da_q_norm, MXU_DIM // DA_HEAD_DIM)),
        "k_gain": row(jnp.tile(da_k_norm, MXU_DIM // DA_HEAD_DIM)),
        "slopes": (2.0 ** (-8.0 * jnp.arange(1, DA_HEADS + 1, dtype=F32) / DA_HEADS)) * LOG2E,
        "lam_params": jnp.stack([lambda_q1, lambda_k1, lambda_q2, lambda_k2]).astype(F32),
        "subln": row(da_subln),
        "conv_w": jnp.pad(gdn_conv.astype(F32), ((0, 8 - CONV_K), (0, 0))),
        "alog_row": jnp.pad(gdn_a_log.reshape(1, -1).astype(F32), ((0, 0), (0, LANES - 2 * GDN_HEADS))),
        "dtb_row": jnp.pad(gdn_dt_bias.reshape(1, -1).astype(F32), ((0, 0), (0, LANES - 2 * GDN_HEADS))),
        "ltri": _block_diag_mask(GDN_BLOCK, GDN_CHUNK, "lower"),
        "utri": _block_diag_mask(GDN_BLOCK, GDN_CHUNK, "upper"),
        "ones_bd": _block_diag_mask(GDN_BLOCK, GDN_CHUNK, "full"),
        "out_norm": row(gdn_out_norm),
        "w_branch_a": w_branch_a.astype(BF16),
        "w_branch_b": w_branch_b.astype(BF16),
        "w_out": w_out.astype(BF16),
        "norm_ffn": row(norm_ffn),
        "w_ffn_in": w_ffn_in.astype(BF16),
        "w_ffn_out": w_ffn_out.astype(BF16),
        "norm_ple": row(norm_ple),
        "w_ple_gate": w_ple_gate.astype(BF16),
        "w_ple_proj": w_ple_proj.astype(BF16),
    }


def _tile(n, pref):
    return pref if n % pref == 0 else n


def _encoder_layer(x, p, prm):
    b, s, _ = x.shape
    t = b * s
    x2d = x.reshape(t, D_MODEL)
    proj, ab = _stage_a(x2d, prm, _tile(t, 1024))
    proj3 = proj.reshape(b, s, PROJ_MAIN_W)
    ya = _stage_b(proj3, prm, _tile(s, 512), _tile(s, 512))
    gcb, gct = _stage_c0(ab, prm)
    yb = _stage_c(proj3, gcb.reshape(b, s, LANES), gct.reshape(b, s // GDN_BLOCK, LANES, GDN_BLOCK), prm)
    x1 = _stage_d(ya.reshape(t, -1), yb.reshape(t, -1), proj, x2d, prm, _tile(t, 512))
    x2 = _stage_e(x1, prm, _tile(t, 1024), 512, 256)
    x3 = _stage_f(x2, p.reshape(t, PLE_DIM), prm, _tile(t, 512))
    return x3.reshape(b, s, D_MODEL)


def kernel(x_prompt, x_sample, p_prompt, p_sample, norm_mix, w_in, da_q_norm, da_k_norm, lambda_q1, lambda_k1, lambda_q2, lambda_k2, da_subln, gdn_conv, gdn_a_log, gdn_dt_bias, gdn_out_norm, w_branch_a, w_branch_b, w_out, norm_ffn, w_ffn_in, w_ffn_out, norm_ple, w_ple_gate, w_ple_proj):
    layer_params = (norm_mix, w_in, da_q_norm, da_k_norm, lambda_q1, lambda_k1, lambda_q2, lambda_k2,
                    da_subln, gdn_conv, gdn_a_log, gdn_dt_bias, gdn_out_norm, w_branch_a, w_branch_b,
                    w_out, norm_ffn, w_ffn_in, w_ffn_out, norm_ple, w_ple_gate, w_ple_proj)
    depth = norm_mix.shape[0]
    assert depth == 1, "LAM_INIT is the depth-0 value"
    y_prompt, y_sample = x_prompt, x_sample
    for layer in range(depth):
        prm = _prepare_params(*(w[layer] for w in layer_params))
        y_prompt = _encoder_layer(y_prompt, p_prompt[layer], prm)
        y_sample = _encoder_layer(y_sample, p_sample[layer], prm)
    return (y_prompt, y_sample)
```

```python
import functools
import math

import jax
import jax.numpy as jnp
from jax import lax
from jax.experimental import pallas as pl
from jax.experimental.pallas import tpu as pltpu

F32 = jnp.float32
BF16 = jnp.bfloat16

D_MODEL = 2048
DA_HEADS = 8
DA_HEAD_DIM = 64
GDN_HEADS = 8
GDN_DIM = 128
GDN_CHUNK = 64
CONV_K = 5
FFN_HIDDEN = 5632
PLE_DIM = 256
NORM_EPS = 1e-6
LAM_INIT = 0.8 - 0.6 * math.exp(-0.3 * 0)
LOG2E = math.log2(math.e)

LANES = 128
MXU_DIM = 256
GDN_BLOCK = 256
GDN_SINGLE_BUFFER_ABOVE = 4096
VMEM_LIMIT = 56 * 1024 * 1024
BOUND_SLACK = 1.01
SKIP_LOG2 = 160.0

COL_Q, COL_K, COL_V = 0, 8, 16
COL_GQ, COL_GK, COL_GV, COL_GZ = 24, 32, 40, 48
PROJ_MAIN_W = 11264
PROJ_TN = 1024
PROJ_FIXED_TILES = (COL_GZ + 8) * LANES // PROJ_TN

NT_DIMS = (((1,), (1,)), ((), ()))
TN_DIMS = (((0,), (0,)), ((), ()))


def _cparams(*sem):
    return pltpu.CompilerParams(dimension_semantics=sem, vmem_limit_bytes=VMEM_LIMIT)


def _const_spec(shape):
    nd = len(shape)
    return pl.BlockSpec(shape, lambda *_: (0,) * nd, pipeline_mode=pl.Buffered(1))


def _rmsnorm_rows(x, gain):
    ms = jnp.mean(x * x, axis=-1, keepdims=True)
    return x * lax.rsqrt(ms + NORM_EPS) * gain


def _sigmoid(x):
    return 1.0 / (1.0 + jnp.exp(-x))


def _split3(x):
    hi = x.astype(BF16)
    r1 = x - hi.astype(F32)
    mid = r1.astype(BF16)
    lo = (r1 - mid.astype(F32)).astype(BF16)
    return hi, mid, lo


def _dot(a, b):
    return jnp.dot(a, b, preferred_element_type=F32)


def _proj_kernel(x_ref, g_ref, w_ref, wgate_ref, wab_ref, grp_ref, qg_ref, kg_ref, o_ref, ab_ref, hn_ref):
    j = pl.program_id(1)

    @pl.when(j == 0)
    def _():
        hb = _rmsnorm_rows(x_ref[...], g_ref[...]).astype(BF16)
        hn_ref[...] = hb
        ab_ref[...] = _dot(hb, wab_ref[...])

    @pl.when(j < 2)
    def _():
        acc = _dot(hn_ref[...], w_ref[...])
        gain = jnp.where(j == 0, qg_ref[...] * (DA_HEAD_DIM ** -0.5 * LOG2E), kg_ref[...])
        for c in range(PROJ_TN // MXU_DIM):
            a = acc[:, c * MXU_DIM:(c + 1) * MXU_DIM]
            sq = a * a
            hi = sq.astype(BF16)
            lo = (sq - hi.astype(F32)).astype(BF16)
            ss = _dot(hi, grp_ref[...]) + _dot(lo, grp_ref[...])
            y = a * lax.rsqrt(ss * (1.0 / DA_HEAD_DIM) + NORM_EPS) * gain
            o_ref[:, c * MXU_DIM:(c + 1) * MXU_DIM] = y.astype(BF16)

    @pl.when((j >= 2) & (j < PROJ_FIXED_TILES))
    def _():
        o_ref[...] = _dot(hn_ref[...], w_ref[...]).astype(BF16)

    @pl.when(j >= PROJ_FIXED_TILES)
    def _():
        o_ref[...] = _sigmoid(_dot(hn_ref[...], wgate_ref[...])).astype(BF16)


def _stage_a(x2d, prm, tm):
    t = x2d.shape[0]
    return pl.pallas_call(
        _proj_kernel,
        grid=(t // tm, PROJ_MAIN_W // PROJ_TN),
        in_specs=[
            pl.BlockSpec((tm, D_MODEL), lambda i, j: (i, 0)),
            _const_spec((1, D_MODEL)),
            pl.BlockSpec((D_MODEL, PROJ_TN), lambda i, j: (0, jnp.minimum(j, PROJ_FIXED_TILES - 1))),
            pl.BlockSpec((D_MODEL, PROJ_TN), lambda i, j: (0, jnp.maximum(j - PROJ_FIXED_TILES, 0))),
            _const_spec((D_MODEL, LANES)),
            _const_spec((MXU_DIM, MXU_DIM)),
            _const_spec((1, MXU_DIM)),
            _const_spec((1, MXU_DIM)),
        ],
        out_specs=[
            pl.BlockSpec((tm, PROJ_TN), lambda i, j: (i, j)),
            pl.BlockSpec((tm, LANES), lambda i, j: (i, 0)),
        ],
        out_shape=[
            jax.ShapeDtypeStruct((t, PROJ_MAIN_W), BF16),
            jax.ShapeDtypeStruct((t, LANES), F32),
        ],
        scratch_shapes=[pltpu.VMEM((tm, D_MODEL), BF16)],
        compiler_params=_cparams("parallel", "arbitrary"),
        name="stage_a_proj",
    )(x2d, prm["norm_mix"], prm["w_fixed"], prm["w_gates"], prm["w_ab"], prm["grp64"], prm["q_gain"], prm["k_gain"])


def _attn_kernel(slopes_ref, q_ref, k_ref, v_ref, lam_ref, subln_ref, qg_ref, kg_ref, o_ref,
                 kx_ref, vx_ref, qx_ref, sa_ref, sb_ref, m_ref, acc_ref, dg_ref, *, tq, tk, seq):
    h = pl.program_id(1)
    qi = pl.program_id(2)
    slope = slopes_ref[h]

    def split3_f32(x):
        return tuple(p.astype(F32) for p in _split3(x))

    def ext_lanes(shape, pieces, unit):
        lane = lax.broadcasted_iota(jnp.int32, shape, 1)
        hi, mid, lo = pieces
        return jnp.where(lane == 0, hi, jnp.where(lane == 1, mid, jnp.where(lane == 2, lo,
                         jnp.where(lane < 6, unit, 0.0)))).astype(BF16)

    @pl.when(qi == 0)
    def _():
        def fill(i, c):
            r0 = pl.multiple_of(i * tk, tk)
            pos = (r0 + lax.broadcasted_iota(jnp.int32, (tk, LANES), 0)).astype(F32)
            hi, mid, lo = split3_f32(slope * pos)
            lane = lax.broadcasted_iota(jnp.int32, (tk, LANES), 1)
            kext = jnp.where(lane < 3, 1.0, jnp.where(lane == 3, hi, jnp.where(lane == 4, mid,
                             jnp.where(lane == 5, lo, 0.0)))).astype(BF16)
            kx_ref[pl.ds(r0, tk), 0:LANES] = k_ref[0, pl.ds(r0, tk), :]
            kx_ref[pl.ds(r0, tk), LANES:2 * LANES] = kext
            vx_ref[pl.ds(r0, tk), 0:LANES] = v_ref[0, pl.ds(r0, tk), :]
            vx_ref[pl.ds(r0, tk), LANES:2 * LANES] = jnp.ones((tk, LANES), BF16)
            return c

        lax.fori_loop(0, seq // tk, fill, 0)
        rr = lax.broadcasted_iota(jnp.int32, (2 * tq, tk), 0)
        rr = jnp.where(rr >= tq, rr - tq, rr)
        cc = lax.broadcasted_iota(jnp.int32, (2 * tq, tk), 1)
        dg_ref[...] = 2.0 * jnp.maximum(cc - rr, 0).astype(F32)

    q0 = qi * tq
    q = q_ref[0]
    lane = lax.broadcasted_iota(jnp.int32, (tq, LANES), 1)
    qm0 = jnp.where(lane < DA_HEAD_DIM, q, jnp.zeros_like(q))
    qm1 = jnp.where(lane >= DA_HEAD_DIM, q, jnp.zeros_like(q))
    pos = (q0 + lax.broadcasted_iota(jnp.int32, (tq, LANES), 0)).astype(F32)
    hi, mid, lo = split3_f32(slope * pos)
    ext_l = ext_lanes((tq, LANES), (-hi, -mid, -lo), 1.0)
    ext_r = ext_lanes((tq, LANES), (hi, mid, lo), -1.0)
    for side, ext in ((0, ext_l), (1, ext_r)):
        qx_ref[side, 0:tq, 0:LANES] = qm0
        qx_ref[side, tq:2 * tq, 0:LANES] = qm1
        qx_ref[side, 0:tq, LANES:2 * LANES] = ext
        qx_ref[side, tq:2 * tq, LANES:2 * LANES] = ext
    m_ref[...] = jnp.full(m_ref.shape, -1e30, F32)
    acc_ref[...] = jnp.zeros(acc_ref.shape, F32)

    nk = seq // tk
    n_left = q0 // tk

    gq = jnp.max(jnp.abs(qg_ref[...]), axis=-1, keepdims=True)
    gk = jnp.max(jnp.abs(kg_ref[...]), axis=-1, keepdims=True)
    bound = (DA_HEAD_DIM * DA_HEAD_DIM ** -0.5 * LOG2E * BOUND_SLACK) * gq * gk
    w_f = jnp.minimum((SKIP_LOG2 + 2.0 * bound) / slope, float(seq))
    w = jnp.ceil(w_f).astype(jnp.int32)[0, 0]
    lo = jnp.minimum(jnp.maximum(q0 - w, 0) // tk, n_left)
    hi = jnp.minimum((w + q0 + tq - 2) // tk + 1, nk)
    odd = (hi - lo) % 2 == 1
    hi = jnp.where(odd & (lo == 0), hi + 1, hi)
    lo = jnp.where(odd & (lo > 0), lo - 1, lo)
    n_pairs = (hi - lo) // 2

    def kblock(ki):
        return kx_ref[pl.ds(pl.multiple_of(ki * tk, tk), tk), :]

    def scores(ki):
        side = jnp.where(ki >= n_left, 1, 0)
        return lax.dot_general(qx_ref[side], kblock(ki), NT_DIMS, preferred_element_type=F32)

    def consume(s, ki):
        m_old = m_ref[...]
        m_new = jnp.maximum(m_old, jnp.max(s, axis=-1, keepdims=True))
        alpha = jnp.exp2(m_old - m_new)
        p = jnp.exp2(s - jnp.concatenate([m_new] * (tk // LANES), axis=1))
        vx = vx_ref[pl.ds(pl.multiple_of(ki * tk, tk), tk), :]
        pv = _dot(p.astype(BF16), vx)
        acc_ref[...] = jnp.concatenate([alpha, alpha], axis=1) * acc_ref[...] + pv
        m_ref[...] = m_new

    def nth(j):
        i = lo + j - 1
        return jnp.where(i >= n_left, i + 1, i)

    sa_ref[...] = (lax.dot_general(qx_ref[0], kblock(n_left), NT_DIMS, preferred_element_type=F32)
                   - slope * dg_ref[...])
    if nk == 1:
        consume(sa_ref[...], n_left)
    else:
        def pair(t, last):
            first = jnp.where(t == 0, n_left, nth(2 * t))
            sb_ref[...] = scores(nth(2 * t + 1))
            consume(sa_ref[...], first)
            if not last:
                sa_ref[...] = scores(nth(2 * t + 2))
            consume(sb_ref[...], nth(2 * t + 1))

        def body(t, carry):
            pair(t, False)
            return carry

        lax.fori_loop(0, n_pairs - 1, body, 0)
        pair(n_pairs - 1, True)

    acc = acc_ref[...]
    o = acc[:, 0:LANES] * (1.0 / acc[:, LANES:2 * LANES])
    lp = lam_ref[...]
    lam = (jnp.exp(jnp.sum(lp[0:1, :] * lp[1:2, :], axis=-1, keepdims=True))
           - jnp.exp(jnp.sum(lp[2:3, :] * lp[3:4, :], axis=-1, keepdims=True)) + LAM_INIT)
    d = o[0:tq, :] - lam * o[tq:2 * tq, :]
    y = _rmsnorm_rows(d, subln_ref[...]) * (1.0 - LAM_INIT)
    o_ref[0] = y.astype(BF16)


def _stage_b(proj3, prm, tq, tk):
    b, s, _ = proj3.shape
    assert tk == tq and s % tk == 0, "the diagonal key block is aligned with its query block"
    assert (s // tk) % 2 == 0 or s == tk, "key blocks are consumed in pairs"
    kern = functools.partial(_attn_kernel, tq=tq, tk=tk, seq=s)
    return pl.pallas_call(
        kern,
        grid=(b, DA_HEADS, s // tq),
        in_specs=[
            pl.BlockSpec(memory_space=pltpu.SMEM),
            pl.BlockSpec((1, tq, LANES), lambda bi, h, qi: (bi, qi, COL_Q + h)),
            pl.BlockSpec((1, s, LANES), lambda bi, h, qi: (bi, 0, COL_K + h)),
            pl.BlockSpec((1, s, LANES), lambda bi, h, qi: (bi, 0, COL_V + h)),
            _const_spec((4, DA_HEAD_DIM)),
            _const_spec((1, LANES)),
            _const_spec((1, MXU_DIM)),
            _const_spec((1, MXU_DIM)),
        ],
        out_specs=pl.BlockSpec((1, tq, LANES), lambda bi, h, qi: (bi, qi, h)),
        out_shape=jax.ShapeDtypeStruct((b, s, DA_HEADS * LANES), BF16),
        scratch_shapes=[
            pltpu.VMEM((s, 2 * LANES), BF16),
            pltpu.VMEM((s, 2 * LANES), BF16),
            pltpu.VMEM((2, 2 * tq, 2 * LANES), BF16),
            pltpu.VMEM((2 * tq, tk), F32),
            pltpu.VMEM((2 * tq, tk), F32),
            pltpu.VMEM((2 * tq, LANES), F32),
            pltpu.VMEM((2 * tq, 2 * LANES), F32),
            pltpu.VMEM((2 * tq, tk), F32),
        ],
        compiler_params=_cparams("parallel", "parallel", "arbitrary"),
        name="stage_b_diff_attn",
    )(prm["slopes"], proj3, proj3, proj3, prm["lam_params"], prm["subln"], prm["q_gain"], prm["k_gain"])


def _softplus(x):
    return jnp.maximum(x, 0.0) + jnp.log1p(jnp.exp(-jnp.abs(x)))


def _gdn_gate_kernel(ab_ref, alog_ref, dtb_ref, ltri_ref, utri_ref, ones_ref, o_ref, ot_ref):
    x = ab_ref[...]
    g = -jnp.exp(alog_ref[...]) * _softplus(x + dtb_ref[...])
    parts = _split3(g)
    cf = sum(_dot(ltri_ref[...], p) for p in parts)
    cb = sum(_dot(utri_ref[...], p) for p in parts)
    tot = sum(_dot(ones_ref[...], p) for p in parts)
    lane = lax.broadcasted_iota(jnp.int32, x.shape, 1)
    out = jnp.where(lane < 8, cf,
                    jnp.where(lane < 16, cb,
                              jnp.where(lane < 32, _sigmoid(x), pltpu.roll(tot, 32, axis=1))))
    o_ref[...] = out
    ot_ref[0] = out.T


def _stage_c0(ab2d, prm):
    t = ab2d.shape[0]
    nb = t // GDN_BLOCK
    return pl.pallas_call(
        _gdn_gate_kernel,
        grid=(nb,),
        in_specs=[
            pl.BlockSpec((GDN_BLOCK, LANES), lambda i: (i, 0)),
            _const_spec((1, LANES)),
            _const_spec((1, LANES)),
            _const_spec((GDN_BLOCK, GDN_BLOCK)),
            _const_spec((GDN_BLOCK, GDN_BLOCK)),
            _const_spec((GDN_BLOCK, GDN_BLOCK)),
        ],
        out_specs=[
            pl.BlockSpec((GDN_BLOCK, LANES), lambda i: (i, 0)),
            pl.BlockSpec((1, LANES, GDN_BLOCK), lambda i: (i, 0, 0)),
        ],
        out_shape=[
            jax.ShapeDtypeStruct((t, LANES), F32),
            jax.ShapeDtypeStruct((nb, LANES, GDN_BLOCK), F32),
        ],
        compiler_params=_cparams("parallel"),
        name="stage_c0_gdn_gates",
    )(ab2d, prm["alog_row"], prm["dtb_row"], prm["ltri"], prm["utri"], prm["ones_bd"])


def _gdn_kernel(qp_ref, kp_ref, vp_ref, z_ref, wq_ref, wk_ref, wv_ref, gcb_ref, gct_ref, onorm_ref,
                o_ref, xp_ref, qn_ref, kn_ref, vn_ref, of_ref, st_ref, xs_ref, bs_ref, oi_ref, eg_ref, *, seq):
    h = pl.program_id(1)
    nblk = seq // GDN_BLOCK
    nch = GDN_BLOCK // GDN_CHUNK
    pad = 8
    rows = 2 * GDN_BLOCK

    def conv_phase(src_ref, w_ref, dst_ref, mode):
        xp_ref[0:pad, :] = jnp.zeros((pad, LANES), F32)
        xp_ref[pad + seq:pad + seq + pad, :] = jnp.zeros((pad, LANES), F32)

        def fill(i, c):
            r0 = pl.multiple_of(i * rows, rows)
            xp_ref[pl.ds(pad + r0, rows), :] = src_ref[0, pl.ds(r0, rows), :].astype(F32)
            return c

        lax.fori_loop(0, seq // rows, fill, 0)

        def conv(i, c):
            r0 = pl.multiple_of(i * rows, rows)
            acc = jnp.zeros((rows, LANES), F32)
            for t in range(CONV_K):
                acc = acc + w_ref[t:t + 1, :] * xp_ref[pl.ds(r0 + pad + t - (CONV_K - 1) // 2, rows), :]
            y = acc * _sigmoid(acc)
            if mode != "v":
                y = y * lax.rsqrt(jnp.sum(y * y, axis=-1, keepdims=True) + NORM_EPS)
            if mode == "q":
                y = y * (GDN_DIM ** -0.5)
            dst_ref[pl.ds(r0, rows), :] = y
            return c

        lax.fori_loop(0, seq // rows, conv, 0)

    conv_phase(qp_ref, wq_ref, qn_ref, "q")
    conv_phase(kp_ref, wk_ref, kn_ref, "k")
    conv_phase(vp_ref, wv_ref, vn_ref, "v")

    st_ref[...] = jnp.zeros(st_ref.shape, F32)

    ri = lax.broadcasted_iota(jnp.int32, (GDN_BLOCK, GDN_BLOCK), 0)
    ci = lax.broadcasted_iota(jnp.int32, (GDN_BLOCK, GDN_BLOCK), 1)
    same = (ri // GDN_CHUNK) == (ci // GDN_CHUNK)
    eye = (ri == ci).astype(F32)
    lane = lax.broadcasted_iota(jnp.int32, (GDN_BLOCK, LANES), 1)

    def pick(gblk, col):
        return jnp.sum(jnp.where(lane == col, gblk, 0.0), axis=-1, keepdims=True)

    def prep(blk, direction, slot):
        r0 = pl.multiple_of(blk * GDN_BLOCK, GDN_BLOCK)
        q = qn_ref[pl.ds(r0, GDN_BLOCK), :]
        k = kn_ref[pl.ds(r0, GDN_BLOCK), :]
        v = vn_ref[pl.ds(r0, GDN_BLOCK), :]
        gblk = gcb_ref[0, pl.ds(r0, GDN_BLOCK), :]
        col = direction * GDN_HEADS + h
        gcol = pick(gblk, col)
        bcol = pick(gblk, 16 + col)
        tcol = pick(gblk, 32 + col)
        grow = gct_ref[0, blk, pl.ds(col, 1), :]
        if direction == 0:
            incl, strict = same & (ci <= ri), same & (ci < ri)
        else:
            incl, strict = same & (ci >= ri), same & (ci > ri)
        decay = jnp.where(incl, jnp.exp(jnp.where(incl, gcol - grow, 0.0)), 0.0)

        kbf = k.astype(BF16)
        kb = k * bcol
        eg = jnp.exp(gcol)
        rhs = jnp.concatenate([v * bcol, kb * eg], axis=1)
        kdec = (k * jnp.exp(tcol - gcol)).astype(BF16)
        egl = jnp.broadcast_to(jnp.exp(tcol), (GDN_BLOCK, LANES))
        kq = lax.dot_general(jnp.concatenate([kb, q], axis=0).astype(BF16), kbf, NT_DIMS,
                             preferred_element_type=F32)
        yield
        a = jnp.where(strict, kq[0:GDN_BLOCK, :] * decay, 0.0)
        qkm = (kq[GDN_BLOCK:, :] * decay).astype(BF16)
        assert GDN_CHUNK == 64
        a1 = a.astype(BF16)
        ima = eye - a
        a2f = _dot(a1, a1)
        yield
        a2 = a2f.astype(BF16)
        a4f = _dot(a2, a2)
        p1 = ima + _dot(ima.astype(BF16), a2)
        yield
        a4 = a4f.astype(BF16)
        a8f = _dot(a4, a4)
        yield
        a8 = a8f.astype(BF16)
        p2 = eye + a4f + a8f + _dot(a4, a8)
        a16 = _dot(a8, a8).astype(BF16)
        yield
        a32 = _dot(a16, a16).astype(BF16)
        p12 = _dot(p1.astype(BF16), p2.astype(BF16))
        x1 = rhs + _dot(a16, rhs.astype(BF16))
        yield
        x2 = x1 + _dot(a32, x1.astype(BF16))
        yield
        solb = _dot(p12.astype(BF16), x2.astype(BF16)).astype(BF16)
        yield
        intra = _dot(qkm, solb)
        bks = [lax.dot_general(kdec[c * GDN_CHUNK:(c + 1) * GDN_CHUNK, :], solb[c * GDN_CHUNK:(c + 1) * GDN_CHUNK, :],
                               TN_DIMS, preferred_element_type=F32) for c in range(nch)]
        yield
        eg_ref[slot] = egl
        oi_ref[slot] = intra[:, 0:GDN_DIM]
        qt = (q * eg - intra[:, GDN_DIM:]).astype(BF16)
        for c in range(nch):
            xs_ref[slot, c, 0:GDN_DIM, :] = bks[c][:, GDN_DIM:].astype(BF16)
            xs_ref[slot, c, GDN_DIM:GDN_DIM + GDN_CHUNK, :] = qt[c * GDN_CHUNK:(c + 1) * GDN_CHUNK, :]
            bs_ref[slot, c] = bks[c][:, 0:GDN_DIM]

    def recur_chunk(blk, direction, slot, c, out_ref):
        state = st_ref[direction]
        y = _dot(xs_ref[slot, c], state.astype(BF16))
        r0 = pl.multiple_of(blk * GDN_BLOCK, GDN_BLOCK)
        out_ref[pl.ds(r0 + c * GDN_CHUNK, GDN_CHUNK), :] = y[GDN_DIM:, :] + oi_ref[slot, c * GDN_CHUNK:(c + 1) * GDN_CHUNK, :]
        st_ref[direction] = state * eg_ref[slot, c * GDN_CHUNK:c * GDN_CHUNK + 1, :] + bs_ref[slot, c] - y[0:GDN_DIM, :]

    def step_blocks(j):
        return ((2 * j, 0, 0), (2 * j + 1, 0, 1), (nblk - 1 - 2 * j, 1, 2), (nblk - 2 - 2 * j, 1, 3))

    def recur_steps(j):
        (f0, _, _), (f1, _, _), (b0, _, _), (b1, _, _) = step_blocks(j)
        for fb, bb, fs, bs in ((f0, b0, 0, 2), (f1, b1, 1, 3)):
            for c in range(nch):
                yield lambda fb=fb, fs=fs, c=c: recur_chunk(fb, 0, fs, c, of_ref)
                yield lambda bb=bb, bs=bs, c=c: recur_chunk(bb, 1, bs, nch - 1 - c, xp_ref)

    def emit(prep_step, recur_step):
        gens = [prep(*b) for b in step_blocks(prep_step)] if prep_step is not None else []
        rec = list(recur_steps(recur_step)) if recur_step is not None else []
        n_levels = 9
        per_level = -(-len(rec) // (n_levels - 1)) if gens else len(rec)
        live = True
        while live or rec:
            live = False
            for g in gens:
                try:
                    next(g)
                    live = True
                except StopIteration:
                    pass
            last_level = bool(gens) and not live
            if not last_level:
                for f in rec[:per_level]:
                    f()
                rec = rec[per_level:]
            assert not (last_level and rec), "recurrence reads must all precede the scratch overwrite"

    nstep = nblk // 2
    emit(0, None)

    def main(j, c):
        emit(j, j - 1)
        return c

    lax.fori_loop(1, nstep, main, 0)
    emit(None, nstep - 1)

    def fin(i, c):
        r0 = pl.multiple_of(i * rows, rows)
        o = of_ref[pl.ds(r0, rows), :] + xp_ref[pl.ds(r0, rows), :]
        z = z_ref[0, pl.ds(r0, rows), :].astype(F32)
        y = _rmsnorm_rows(o, onorm_ref[...]) * (z * _sigmoid(z))
        o_ref[0, pl.ds(r0, rows), :] = y.astype(BF16)
        return c

    lax.fori_loop(0, seq // rows, fin, 0)


def _stage_c(proj3, gcb3, gct4, prm):
    b, s, _ = proj3.shape
    nblk = s // GDN_BLOCK
    nch = GDN_BLOCK // GDN_CHUNK
    assert nblk % 2 == 0

    def once(shape, imap):
        if s > GDN_SINGLE_BUFFER_ABOVE:
            return pl.BlockSpec(shape, imap, pipeline_mode=pl.Buffered(1))
        return pl.BlockSpec(shape, imap)

    kern = functools.partial(_gdn_kernel, seq=s)
    return pl.pallas_call(
        kern,
        grid=(b, GDN_HEADS),
        in_specs=[
            once((1, s, LANES), lambda bi, h: (bi, 0, COL_GQ + h)),
            once((1, s, LANES), lambda bi, h: (bi, 0, COL_GK + h)),
            once((1, s, LANES), lambda bi, h: (bi, 0, COL_GV + h)),
            once((1, s, LANES), lambda bi, h: (bi, 0, COL_GZ + h)),
            pl.BlockSpec((8, LANES), lambda bi, h: (0, h)),
            pl.BlockSpec((8, LANES), lambda bi, h: (0, GDN_HEADS + h)),
            pl.BlockSpec((8, LANES), lambda bi, h: (0, 2 * GDN_HEADS + h)),
            once((1, s, LANES), lambda bi, h: (bi, 0, 0)),
            once((1, nblk, 2 * GDN_HEADS, GDN_BLOCK), lambda bi, h: (bi, 0, 0, 0)),
            _const_spec((1, LANES)),
        ],
        out_specs=pl.BlockSpec((1, s, LANES), lambda bi, h: (bi, 0, h)),
        out_shape=jax.ShapeDtypeStruct((b, s, GDN_HEADS * GDN_DIM), BF16),
        scratch_shapes=[
            pltpu.VMEM((s + 16, LANES), F32),
            pltpu.VMEM((s, LANES), F32),
            pltpu.VMEM((s, LANES), F32),
            pltpu.VMEM((s, LANES), F32),
            pltpu.VMEM((s, LANES), F32),
            pltpu.VMEM((2, GDN_DIM, GDN_DIM), F32),
            pltpu.VMEM((4, nch, GDN_DIM + GDN_CHUNK, GDN_DIM), BF16),
            pltpu.VMEM((4, nch, GDN_DIM, GDN_DIM), F32),
            pltpu.VMEM((4, GDN_BLOCK, GDN_DIM), F32),
            pltpu.VMEM((4, GDN_BLOCK, LANES), F32),
        ],
        compiler_params=_cparams("parallel", "arbitrary"),
        name="stage_c_gdn",
    )(proj3, proj3, proj3, proj3, prm["conv_w"], prm["conv_w"], prm["conv_w"], gcb3, gct4, prm["out_norm"])


def _merge_kernel(ya_ref, yb_ref, ga0_ref, ga1_ref, gb0_ref, gb1_ref, x_ref, wa_ref, wb_ref, wo_ref, o_ref):
    half = D_MODEL // 2
    pa = _dot(ya_ref[...], wa_ref[...])
    pb = _dot(yb_ref[...], wb_ref[...])
    m0 = ga0_ref[...].astype(F32) * pa[:, 0:half] + gb0_ref[...].astype(F32) * pb[:, 0:half]
    m1 = ga1_ref[...].astype(F32) * pa[:, half:] + gb1_ref[...].astype(F32) * pb[:, half:]
    mixed = jnp.concatenate([m0, m1], axis=1).astype(BF16)
    o_ref[...] = x_ref[...] + _dot(mixed, wo_ref[...])


def _stage_d(ya2d, yb2d, proj2d, x2d, prm, tm):
    t = x2d.shape[0]
    half = D_MODEL // 2
    gate0 = (COL_GZ + GDN_HEADS) * LANES // half
    row = lambda c: pl.BlockSpec((tm, half), lambda i: (i, c))
    return pl.pallas_call(
        _merge_kernel,
        grid=(t // tm,),
        in_specs=[
            row(0), row(0), row(gate0), row(gate0 + 1), row(gate0 + 2), row(gate0 + 3),
            pl.BlockSpec((tm, D_MODEL), lambda i: (i, 0)),
            _const_spec((half, D_MODEL)),
            _const_spec((half, D_MODEL)),
            _const_spec((D_MODEL, D_MODEL)),
        ],
        out_specs=pl.BlockSpec((tm, D_MODEL), lambda i: (i, 0)),
        out_shape=jax.ShapeDtypeStruct((t, D_MODEL), F32),
        compiler_params=_cparams("parallel"),
        name="stage_d_merge",
    )(ya2d, yb2d, proj2d, proj2d, proj2d, proj2d, x2d, prm["w_branch_a"], prm["w_branch_b"], prm["w_out"])


def _ffn_kernel(x_ref, g_ref, wu_ref, wg_ref, wo_ref, o_ref, hn_ref, act_ref, *, tf, tn, nf):
    j = pl.program_id(1)

    @pl.when(j == 0)
    def _():
        hn_ref[...] = _rmsnorm_rows(x_ref[...], g_ref[...]).astype(BF16)

    @pl.when(j < nf)
    def _():
        hn = hn_ref[...]
        up = _dot(hn, wu_ref[...])
        gt = _dot(hn, wg_ref[...])
        act_ref[:, pl.ds(pl.multiple_of(j * tf, tf), tf)] = (gt * _sigmoid(gt) * up).astype(BF16)

    @pl.when(j >= nf)
    def _():
        c0 = pl.multiple_of((j - nf) * tn, tn)
        o_ref[...] = x_ref[:, pl.ds(c0, tn)] + _dot(act_ref[...], wo_ref[...])


def _stage_e(x2d, prm, tm, tf, tn):
    t = x2d.shape[0]
    nf = FFN_HIDDEN // tf
    kern = functools.partial(_ffn_kernel, tf=tf, tn=tn, nf=nf)
    return pl.pallas_call(
        kern,
        grid=(t // tm, nf + D_MODEL // tn),
        in_specs=[
            pl.BlockSpec((tm, D_MODEL), lambda i, j: (i, 0)),
            _const_spec((1, D_MODEL)),
            pl.BlockSpec((D_MODEL, tf), lambda i, j: (0, jnp.minimum(j, nf - 1))),
            pl.BlockSpec((D_MODEL, tf), lambda i, j: (0, nf + jnp.minimum(j, nf - 1))),
            pl.BlockSpec((FFN_HIDDEN, tn), lambda i, j: (0, jnp.maximum(j - nf, 0))),
        ],
        out_specs=pl.BlockSpec((tm, tn), lambda i, j: (i, jnp.maximum(j - nf, 0))),
        out_shape=jax.ShapeDtypeStruct((t, D_MODEL), F32),
        scratch_shapes=[pltpu.VMEM((tm, D_MODEL), BF16), pltpu.VMEM((tm, FFN_HIDDEN), BF16)],
        compiler_params=_cparams("parallel", "arbitrary"),
        name="stage_e_ffn",
    )(x2d, prm["norm_ffn"], prm["w_ffn_in"], prm["w_ffn_in"], prm["w_ffn_out"])


def _ple_kernel(x_ref, g_ref, wg_ref, p_ref, wp_ref, o_ref):
    x = x_ref[...]
    hn = _rmsnorm_rows(x, g_ref[...]).astype(BF16)
    gate = _sigmoid(_dot(hn, wg_ref[...]))
    emb = _dot(p_ref[...].astype(BF16), wp_ref[...])
    o_ref[...] = x + gate * emb


def _stage_f(x2d, p2d, prm, tm):
    t = x2d.shape[0]
    return pl.pallas_call(
        _ple_kernel,
        grid=(t // tm,),
        in_specs=[
            pl.BlockSpec((tm, D_MODEL), lambda i: (i, 0)),
            _const_spec((1, D_MODEL)),
            _const_spec((D_MODEL, D_MODEL)),
            pl.BlockSpec((tm, PLE_DIM), lambda i: (i, 0)),
            _const_spec((PLE_DIM, D_MODEL)),
        ],
        out_specs=pl.BlockSpec((tm, D_MODEL), lambda i: (i, 0)),
        out_shape=jax.ShapeDtypeStruct((t, D_MODEL), F32),
        compiler_params=_cparams("parallel"),
        name="stage_f_ple",
    )(x2d, prm["norm_ple"], prm["w_ple_gate"], p2d, prm["w_ple_proj"])


def _block_diag_mask(n, blk, kind):
    r = jnp.arange(n)[:, None]
    c = jnp.arange(n)[None, :]
    same = (r // blk) == (c // blk)
    if kind == "lower":
        same = same & (c <= r)
    elif kind == "upper":
        same = same & (c >= r)
    return same.astype(BF16)


def _prepare_params(norm_mix, w_in, da_q_norm, da_k_norm, lambda_q1, lambda_k1, lambda_q2, lambda_k2,
                    da_subln, gdn_conv, gdn_a_log, gdn_dt_bias, gdn_out_norm, w_branch_a, w_branch_b,
                    w_out, norm_ffn, w_ffn_in, w_ffn_out, norm_ple, w_ple_gate, w_ple_proj):
    n_fixed = COL_GZ * LANES + GDN_HEADS * GDN_DIM
    n_ab = 4 * GDN_HEADS
    row = lambda v: v.reshape(1, -1).astype(F32)
    return {
        "norm_mix": row(norm_mix),
        "w_fixed": w_in[:, :n_fixed].astype(BF16),
        "w_gates": w_in[:, n_fixed + n_ab:].astype(BF16),
        "w_ab": w_in[:, n_fixed:n_fixed + LANES].astype(BF16),
        "grp64": _block_diag_mask(MXU_DIM, DA_HEAD_DIM, "full"),
        "q_gain": row(jnp.tile(da_q_norm, MXU_DIM // DA_HEAD_DIM)),
        "k_gain": row(jnp.tile(da_k_norm, MXU_DIM // DA_HEAD_DIM)),
        "slopes": (2.0 ** (-8.0 * jnp.arange(1, DA_HEADS + 1, dtype=F32) / DA_HEADS)) * LOG2E,
        "lam_params": jnp.stack([lambda_q1, lambda_k1, lambda_q2, lambda_k2]).astype(F32),
        "subln": row(da_subln),
        "conv_w": jnp.pad(gdn_conv.astype(F32), ((0, 8 - CONV_K), (0, 0))),
        "alog_row": jnp.pad(gdn_a_log.reshape(1, -1).astype(F32), ((0, 0), (0, LANES - 2 * GDN_HEADS))),
        "dtb_row": jnp.pad(gdn_dt_bias.reshape(1, -1).astype(F32), ((0, 0), (0, LANES - 2 * GDN_HEADS))),
        "ltri": _block_diag_mask(GDN_BLOCK, GDN_CHUNK, "lower"),
        "utri": _block_diag_mask(GDN_BLOCK, GDN_CHUNK, "upper"),
        "ones_bd": _block_diag_mask(GDN_BLOCK, GDN_CHUNK, "full"),
        "out_norm": row(gdn_out_norm),
        "w_branch_a": w_branch_a.astype(BF16),
        "w_branch_b": w_branch_b.astype(BF16),
        "w_out": w_out.astype(BF16),
        "norm_ffn": row(norm_ffn),
        "w_ffn_in": w_ffn_in.astype(BF16),
        "w_ffn_out": w_ffn_out.astype(BF16),
        "norm_ple": row(norm_ple),
        "w_ple_gate": w_ple_gate.astype(BF16),
        "w_ple_proj": w_ple_proj.astype(BF16),
    }


def _tile(n, pref):
    return pref if n % pref == 0 else n


def _encoder_layer(x, p, prm):
    b, s, _ = x.shape
    t = b * s
    x2d = x.reshape(t, D_MODEL)
    proj, ab = _stage_a(x2d, prm, _tile(t, 1024))
    proj3 = proj.reshape(b, s, PROJ_MAIN_W)
    ya = _stage_b(proj3, prm, _tile(s, 512), _tile(s, 512))
    gcb, gct = _stage_c0(ab, prm)
    yb = _stage_c(proj3, gcb.reshape(b, s, LANES), gct.reshape(b, s // GDN_BLOCK, LANES, GDN_BLOCK), prm)
    x1 = _stage_d(ya.reshape(t, -1), yb.reshape(t, -1), proj, x2d, prm, _tile(t, 512))
    x2 = _stage_e(x1, prm, _tile(t, 1024), 512, 256)
    x3 = _stage_f(x2, p.reshape(t, PLE_DIM), prm, _tile(t, 512))
    return x3.reshape(b, s, D_MODEL)


def kernel(x_prompt, x_sample, p_prompt, p_sample, norm_mix, w_in, da_q_norm, da_k_norm, lambda_q1, lambda_k1, lambda_q2, lambda_k2, da_subln, gdn_conv, gdn_a_log, gdn_dt_bias, gdn_out_norm, w_branch_a, w_branch_b, w_out, norm_ffn, w_ffn_in, w_ffn_out, norm_ple, w_ple_gate, w_ple_proj):
    layer_params = (norm_mix, w_in, da_q_norm, da_k_norm, lambda_q1, lambda_k1, lambda_q2, lambda_k2,
                    da_subln, gdn_conv, gdn_a_log, gdn_dt_bias, gdn_out_norm, w_branch_a, w_branch_b,
                    w_out, norm_ffn, w_ffn_in, w_ffn_out, norm_ple, w_ple_gate, w_ple_proj)
    depth = norm_mix.shape[0]
    assert depth == 1, "LAM_INIT is the depth-0 value"
    y_prompt, y_sample = x_prompt, x_sample
    for layer in range(depth):
        prm = _prepare_params(*(w[layer] for w in layer_params))
        y_prompt = _encoder_layer(y_prompt, p_prompt[layer], prm)
        y_sample = _encoder_layer(y_sample, p_sample[layer], prm)
    return (y_prompt, y_sample)
```
